```python
import jax, jax.numpy as jnp
from jax import lax
import numpy as np

D_MODEL = 1024
BATCH = 2
SEQ = 16384
DEPTH = 4

HEAD_DIM = 64
A_Q_HEADS = 8
A_KV_HEADS = 2
A_WINDOW = 128
B_GROUPS = ((128, 1), (512, 4), (2048, 16))
B_HEADS_PER_GROUP = 4
B_HEADS = B_HEADS_PER_GROUP * len(B_GROUPS)
N_ATTN_HEADS = A_Q_HEADS + B_HEADS
BLOCK = 128
A_Q_W = A_Q_HEADS * HEAD_DIM
A_KV_W = A_KV_HEADS * HEAD_DIM
B_W = B_HEADS * HEAD_DIM
B_OUT_W = B_HEADS_PER_GROUP * HEAD_DIM
IN_SPLITS = (A_Q_W, A_KV_W, A_KV_W, B_W, B_W, B_W, D_MODEL, D_MODEL)
IN_W = sum(IN_SPLITS)
D_FF = ((8 * D_MODEL + 3 * 256 - 1) // (3 * 256)) * 256
DN_ALPHA = (2 * DEPTH) ** 0.25
DN_BETA = (8 * DEPTH) ** -0.25
LN_EPS = 1e-5
NEG_INF = -1e30

kernel_name = "hybrid_swa_sink_dilated_gated_deepnorm"


def layer_norm(x, g, b):
    xf = x.astype(jnp.float32)
    mu = xf.mean(-1, keepdims=True)
    var = jnp.square(xf - mu).mean(-1, keepdims=True)
    y = (xf - mu) * lax.rsqrt(var + LN_EPS)
    return (y * g.astype(jnp.float32) + b.astype(jnp.float32)).astype(x.dtype)


def alibi_slopes(n):
    return jnp.exp2(-8.0 * jnp.arange(1, n + 1, dtype=jnp.float32) / n)


def banded_attention(q, k, v, slopes, max_dist, stride, sinks=None):
    bt, L, H, dh = q.shape
    hkv = k.shape[2]
    G = H // hkv
    nb = -(-L // BLOCK)
    Lp = nb * BLOCK
    q = jnp.pad(q, ((0, 0), (0, Lp - L), (0, 0), (0, 0)))
    kv_pad = ((0, 0), (BLOCK, Lp - L), (0, 0), (0, 0))
    k = jnp.pad(k, kv_pad).reshape(bt, nb + 1, BLOCK, hkv, dh)
    v = jnp.pad(v, kv_pad).reshape(bt, nb + 1, BLOCK, hkv, dh)
    kw = jnp.concatenate([k[:, :-1], k[:, 1:]], axis=2)
    vw = jnp.concatenate([v[:, :-1], v[:, 1:]], axis=2)
    qb = q.reshape(bt, nb, BLOCK, hkv, G, dh)
    s = jnp.einsum('bnqhgd,bnshd->bnhgqs', qb, kw,
                   preferred_element_type=jnp.float32) * (dh ** -0.5)
    qi = jnp.arange(BLOCK)[:, None]
    sj = jnp.arange(2 * BLOCK)[None, :]
    dist = qi + BLOCK - sj
    kpos = jnp.arange(nb)[:, None] * BLOCK + jnp.arange(2 * BLOCK)[None, :] - BLOCK
    valid = ((dist >= 0) & (dist <= max_dist))[None] & (kpos >= 0)[:, None, :]
    bias = -(slopes.astype(jnp.float32).reshape(hkv, G, 1, 1)
             * (dist * stride).astype(jnp.float32))
    s = jnp.where(valid[None, :, None, None], s + bias, NEG_INF)
    m = s.max(-1)
    if sinks is not None:
        sink = sinks.astype(jnp.float32).reshape(1, 1, hkv, G, 1)
        m = jnp.maximum(m, sink)
    e = jnp.exp(s - m[..., None])
    den = e.sum(-1)
    if sinks is not None:
        den = den + jnp.exp(sink - m)
    lse = m + jnp.log(den)
    p = (e / den[..., None]).astype(v.dtype)
    o = jnp.einsum('bnhgqs,bnshd->bnqhgd', p, vw).reshape(bt, Lp, H, dh)[:, :L]
    lse = lse.transpose(0, 1, 4, 2, 3).reshape(bt, Lp, H)[:, :L]
    return o, lse


def dilated_group(q, k, v, slopes, window, dilation):
    b, S, h, dh = q.shape
    n = S // dilation

    def fold(t):
        return t.reshape(b, n, dilation, h, dh).transpose(0, 2, 1, 3, 4).reshape(b * dilation, n, h, dh)

    o, lse = banded_attention(fold(q), fold(k), fold(v), slopes, window // dilation, dilation)
    o = o.reshape(b, dilation, n, h, dh).transpose(0, 2, 1, 3, 4).reshape(b, S, h, dh)
    lse = lse.reshape(b, dilation, n, h).transpose(0, 2, 1, 3).reshape(b, S, h)
    return o, lse


def token_mixer(u, w_in, sinks, w_a, w_b, w_o):
    b, S, _ = u.shape
    idx = list(np.cumsum(IN_SPLITS)[:-1])
    qa, ka, va, qb, kb, vb, ga, gb = jnp.split(u @ w_in, idx, axis=-1)
    slopes = alibi_slopes(N_ATTN_HEADS)
    ya, _ = banded_attention(qa.reshape(b, S, A_Q_HEADS, HEAD_DIM),
                             ka.reshape(b, S, A_KV_HEADS, HEAD_DIM),
                             va.reshape(b, S, A_KV_HEADS, HEAD_DIM),
                             slopes[:A_Q_HEADS], A_WINDOW - 1, 1, sinks)
    ya = ya.reshape(b, S, A_Q_W)
    gshape = (b, S, len(B_GROUPS), B_HEADS_PER_GROUP, HEAD_DIM)
    qb, kb, vb = qb.reshape(gshape), kb.reshape(gshape), vb.reshape(gshape)
    outs, lses = [], []
    for g, (window, dilation) in enumerate(B_GROUPS):
        lo = A_Q_HEADS + g * B_HEADS_PER_GROUP
        o, l = dilated_group(qb[:, :, g], kb[:, :, g], vb[:, :, g],
                             slopes[lo:lo + B_HEADS_PER_GROUP], window, dilation)
        outs.append(o)
        lses.append(l)
    wts = jax.nn.softmax(jnp.stack(lses), axis=0)
    yb = (jnp.stack(outs) * wts[..., None].astype(u.dtype)).sum(0).reshape(b, S, B_OUT_W)
    merged = jax.nn.sigmoid(ga) * (ya @ w_a) + jax.nn.sigmoid(gb) * (yb @ w_b)
    return merged @ w_o


def swiglu(u, w_gate, w_up, w_down):
    return (jax.nn.silu(u @ w_gate) * (u @ w_up)) @ w_down


def setup_inputs(seed: int = 0) -> dict:
    key = jax.random.key(seed)
    ks = jax.random.split(key, 20)
    nrm = lambda k, shape, s: jax.random.normal(k, shape, jnp.float32) * s
    L, D = DEPTH, D_MODEL
    return {
        "x": nrm(ks[0], (BATCH, SEQ, D), 1.0),
        "c": nrm(ks[1], (BATCH, D), 1.0),
        "w_ada": nrm(ks[2], (L, D, 6 * D), 0.5 * D ** -0.5),
        "b_ada": nrm(ks[3], (L, 6 * D), 0.02),
        "w_in": nrm(ks[4], (L, D, IN_W), D ** -0.5),
        "sinks": nrm(ks[5], (L, A_Q_HEADS), 0.5),
        "w_a": nrm(ks[6], (L, A_Q_W, D), A_Q_W ** -0.5),
        "w_b": nrm(ks[7], (L, B_OUT_W, D), B_OUT_W ** -0.5),
        "w_o": nrm(ks[8], (L, D, D), DN_BETA * D ** -0.5),
        "ln1_g": 1.0 + nrm(ks[9], (L, D), 0.02),
        "ln1_b": nrm(ks[10], (L, D), 0.02),
        "w_gate": nrm(ks[11], (L, D, D_FF), D ** -0.5),
        "w_up": nrm(ks[12], (L, D, D_FF), D ** -0.5),
        "w_down": nrm(ks[13], (L, D_FF, D), DN_BETA * D_FF ** -0.5),
        "ln2_g": 1.0 + nrm(ks[14], (L, D), 0.02),
        "ln2_b": nrm(ks[15], (L, D), 0.02),
    }


def reference(x, c, w_ada, b_ada, w_in, sinks, w_a, w_b, w_o, ln1_g, ln1_b,
              w_gate, w_up, w_down, ln2_g, ln2_b):
    sc = jax.nn.silu(c)
    for l in range(DEPTH):
        mod = (sc @ w_ada[l] + b_ada[l])[:, None, :]
        sh1, s1, g1, sh2, s2, g2 = jnp.split(mod, 6, axis=-1)
        u = x * (1 + s1) + sh1
        x = layer_norm(DN_ALPHA * x + g1 * token_mixer(u, w_in[l], sinks[l], w_a[l], w_b[l], w_o[l]),
                       ln1_g[l], ln1_b[l])
        u = x * (1 + s2) + sh2
        x = layer_norm(DN_ALPHA * x + g2 * swiglu(u, w_gate[l], w_up[l], w_down[l]),
                       ln2_g[l], ln2_b[l])
    return x
```

```python
import functools

import numpy as np
import jax
import jax.numpy as jnp
from jax import lax
from jax.experimental import pallas as pl
from jax.experimental.pallas import tpu as pltpu

HEAD_DIM = 64
A_Q_HEADS = 8
A_KV_HEADS = 2
A_WINDOW = 128
B_GROUPS = ((128, 1), (512, 4), (2048, 16))
B_HEADS_PER_GROUP = 4
N_ATTN_HEADS = A_Q_HEADS + B_HEADS_PER_GROUP * len(B_GROUPS)
BLOCK = 128
LN_EPS = 1e-5
NEG_INF = -1e30

LANES = 128
PAIR_W = 2 * HEAD_DIM
A_Q_W = A_Q_HEADS * HEAD_DIM
A_KV_W = A_KV_HEADS * HEAD_DIM
B_GROUP_W = B_HEADS_PER_GROUP * HEAD_DIM
ROW_TILE = 512
ATTN_TOKENS = 2048
VMEM_LIMIT = 56 * 1024 * 1024

F32 = jnp.float32
BF16 = jnp.bfloat16


def _sigmoid(x):
    return 1.0 / (1.0 + jnp.exp(-x))


def _layer_norm(y, g, b):
    mu = jnp.mean(y, axis=-1, keepdims=True)
    yc = y - mu
    var = jnp.mean(yc * yc, axis=-1, keepdims=True)
    return yc * lax.rsqrt(var + LN_EPS) * g + b


def _resident(shape):
    return pl.BlockSpec(shape, lambda *_: (0,) * len(shape), pipeline_mode=pl.Buffered(1))


def _mod_kernel(c_ref, w_ref, b_ref, o_ref):
    c = c_ref[...]
    sc = c * _sigmoid(c)
    o_ref[0] = jnp.dot(sc, w_ref[0], preferred_element_type=F32,
                       precision=lax.Precision.HIGHEST) + b_ref[0]


def _modulation(c, w_ada, b_ada):
    depth, d, n = w_ada.shape
    bsz = c.shape[0]
    rows = 8
    c_pad = jnp.pad(c, ((0, rows - bsz), (0, 0)))
    out = pl.pallas_call(
        _mod_kernel,
        grid=(depth, n // d),
        in_specs=[pl.BlockSpec((rows, d), lambda l, j: (0, 0)),
                  pl.BlockSpec((1, d, d), lambda l, j: (l, 0, j)),
                  pl.BlockSpec((1, 1, d), lambda l, j: (l, 0, j))],
        out_specs=pl.BlockSpec((1, rows, d), lambda l, j: (l, 0, j)),
        out_shape=jax.ShapeDtypeStruct((depth, rows, n), F32),
        compiler_params=pltpu.CompilerParams(
            dimension_semantics=("arbitrary", "arbitrary"), vmem_limit_bytes=VMEM_LIMIT),
        name="adaln_mod",
    )(c_pad, w_ada, b_ada.reshape(depth, 1, n))
    return out[:, :bsz].reshape(depth, bsz, n // d, d)


_INPROJ_GROUPS = ((A_Q_W, False), (2 * A_KV_W, False)) + \
    ((B_GROUP_W, False),) * (3 * len(B_GROUPS)) + ((1024, True), (1024, True))


def _inproj_kernel(x_ref, mod_ref, w_ref, *out_refs):
    sh1 = mod_ref[0, 0:1, :]
    s1 = mod_ref[0, 1:2, :]
    u = (x_ref[...] * (1.0 + s1) + sh1).astype(BF16)
    off = 0
    for ref, (width, gate) in zip(out_refs, _INPROJ_GROUPS):
        r = jnp.dot(u, w_ref[:, off:off + width], preferred_element_type=F32)
        if gate:
            r = _sigmoid(r)
        ref[...] = r.astype(BF16)
        off += width


def _inproj(x2, mod_l, w_in_p, seq):
    t, d = x2.shape
    n = w_in_p.shape[1]
    tiles_per_seq = seq // ROW_TILE
    out_shapes = [jax.ShapeDtypeStruct((t, w), BF16) for w, _ in _INPROJ_GROUPS]
    out_specs = [pl.BlockSpec((ROW_TILE, w), lambda i: (i, 0)) for w, _ in _INPROJ_GROUPS]
    return pl.pallas_call(
        _inproj_kernel,
        grid=(t // ROW_TILE,),
        in_specs=[pl.BlockSpec((ROW_TILE, d), lambda i: (i, 0)),
                  pl.BlockSpec((1, 6, d), lambda i: (i // tiles_per_seq, 0, 0)),
                  _resident((d, n))],
        out_specs=out_specs,
        out_shape=out_shapes,
        compiler_params=pltpu.CompilerParams(
            dimension_semantics=("arbitrary",), vmem_limit_bytes=VMEM_LIMIT),
        name="inproj",
    )(x2, mod_l, w_in_p)


def _fill_bias_tables(tab_ref, slopes_ref, head_ids, max_dist, stride):
    qi = lax.broadcasted_iota(jnp.int32, (BLOCK, 2 * BLOCK), 0)
    sj = lax.broadcasted_iota(jnp.int32, (BLOCK, 2 * BLOCK), 1)
    dist = qi + BLOCK - sj
    valid = (dist >= 0) & (dist <= max_dist)
    valid_first = valid & (sj >= BLOCK)
    dist_f = (dist * stride).astype(F32)
    for h, hid in enumerate(head_ids):
        bias = -(slopes_ref[hid] * dist_f)
        tab_ref[0, h] = jnp.where(valid, bias, NEG_INF)
        tab_ref[1, h] = jnp.where(valid_first, bias, NEG_INF)


def _attend(qm, k2, vext, tab, sink):
    s = lax.dot_general(qm, k2, (((1,), (1,)), ((), ())), preferred_element_type=F32)
    s = s + tab
    m = jnp.max(s, axis=-1, keepdims=True)
    if sink is not None:
        m = jnp.maximum(m, sink)
    e = jnp.exp(s - m).astype(BF16)
    acc = jnp.dot(e, vext, preferred_element_type=F32)
    den = acc[:, LANES:]
    if sink is not None:
        den = den + jnp.exp(sink - m)
    return acc[:, :LANES] / den, m + jnp.log(den)


def _lane_masks():
    lane = lax.broadcasted_iota(jnp.int32, (1, LANES), 1)
    lo = lane < HEAD_DIM
    return lo, lo.astype(BF16), (~lo).astype(BF16)


def _attn_a_kernel(slopes_ref, sinks_ref, q_ref, kvc_ref, kvp_ref, o_ref, tab_ref, kv_ref,
                   *, n_blocks):
    b, j = pl.program_id(0), pl.program_id(1)

    @pl.when((b == 0) & (j == 0))
    def _():
        _fill_bias_tables(tab_ref, slopes_ref, tuple(range(A_Q_HEADS)), A_WINDOW - 1, 1)

    kv_ref[0:BLOCK] = kvp_ref[0]
    kv_ref[BLOCK:] = kvc_ref[0]
    lo, mask_lo, mask_hi = _lane_masks()
    ones = jnp.ones((2 * BLOCK, LANES), BF16)
    group = A_Q_HEADS // A_KV_HEADS

    def body(i, carry):
        row = pl.multiple_of(i * BLOCK, BLOCK)
        first = jnp.where((j == 0) & (i == 0), 1, 0)
        kv = kv_ref[pl.ds(row, 2 * BLOCK), :]
        k2 = kv[:, :LANES]
        vext = jnp.concatenate([kv[:, LANES:], ones], axis=1)
        for p in range(group):
            q2 = q_ref[0, pl.ds(row, BLOCK), p * LANES:(p + 1) * LANES]
            o_lo, _ = _attend(q2 * mask_lo, k2, vext, tab_ref[first, p], sinks_ref[p])
            o_hi, _ = _attend(q2 * mask_hi, k2, vext, tab_ref[first, p + group],
                              sinks_ref[p + group])
            o_ref[0, pl.ds(row, BLOCK), p * LANES:(p + 1) * LANES] = (
                jnp.where(lo, o_lo, o_hi).astype(BF16))
        return carry

    lax.fori_loop(0, n_blocks, body, 0)


def _attention_a(slopes, sinks_l, qa, kva):
    bsz, seq, _ = qa.shape
    qb = ATTN_TOKENS // BLOCK
    kernel = functools.partial(_attn_a_kernel, n_blocks=qb)
    smem = pl.BlockSpec(memory_space=pltpu.SMEM)
    return pl.pallas_call(
        kernel,
        grid=(bsz, seq // ATTN_TOKENS),
        in_specs=[smem, smem,
                  pl.BlockSpec((1, ATTN_TOKENS, A_Q_W), lambda b, j: (b, j, 0)),
                  pl.BlockSpec((1, ATTN_TOKENS, 2 * A_KV_W), lambda b, j: (b, j, 0)),
                  pl.BlockSpec((1, BLOCK, 2 * A_KV_W),
                               lambda b, j: (b, jnp.maximum(j * qb - 1, 0), 0))],
        out_specs=pl.BlockSpec((1, ATTN_TOKENS, A_Q_W), lambda b, j: (b, j, 0)),
        out_shape=jax.ShapeDtypeStruct((bsz, seq, A_Q_W), BF16),
        scratch_shapes=[pltpu.VMEM((2, A_Q_HEADS, BLOCK, 2 * BLOCK), F32),
                        pltpu.VMEM((ATTN_TOKENS + BLOCK, 2 * A_KV_W), BF16)],
        compiler_params=pltpu.CompilerParams(
            dimension_semantics=("arbitrary", "arbitrary"), vmem_limit_bytes=VMEM_LIMIT),
        name="attn_swa",
    )(slopes, sinks_l, qa, kva, kva)


def _attn_b_kernel(slopes_ref, q_ref, kc_ref, kp_ref, vc_ref, vp_ref, o_ref, lse_ref,
                   tab_ref, k_ref, v_ref, *, n_blocks, dilation, max_dist, head0):
    b, j = pl.program_id(0), pl.program_id(1)

    @pl.when((b == 0) & (j == 0))
    def _():
        heads = tuple(range(head0, head0 + B_HEADS_PER_GROUP))
        _fill_bias_tables(tab_ref, slopes_ref, heads, max_dist, dilation)

    k_ref[0:BLOCK] = kp_ref[0]
    k_ref[BLOCK:] = kc_ref[0]
    v_ref[0:BLOCK] = vp_ref[0]
    v_ref[BLOCK:] = vc_ref[0]
    lo, mask_lo, mask_hi = _lane_masks()
    ones = jnp.ones((2 * BLOCK, LANES), BF16)

    def body(i, carry):
        row = pl.multiple_of(i * BLOCK, BLOCK)
        first = jnp.where((j == 0) & (i == 0), 1, 0)
        for r in range(dilation):
            for p in range(B_HEADS_PER_GROUP // 2):
                c0 = r * B_GROUP_W + p * LANES
                q2 = q_ref[0, pl.ds(row, BLOCK), c0:c0 + LANES]
                k2 = k_ref[pl.ds(row, 2 * BLOCK), c0:c0 + LANES]
                vext = jnp.concatenate([v_ref[pl.ds(row, 2 * BLOCK), c0:c0 + LANES], ones],
                                       axis=1)
                o_lo, l_lo = _attend(q2 * mask_lo, k2, vext, tab_ref[first, 2 * p], None)
                o_hi, l_hi = _attend(q2 * mask_hi, k2, vext, tab_ref[first, 2 * p + 1], None)
                o_ref[0, pl.ds(row, BLOCK), c0:c0 + LANES] = jnp.where(lo, o_lo, o_hi).astype(BF16)
                lse_ref[0, pl.ds(row, BLOCK), c0:c0 + LANES] = jnp.where(lo, l_lo, l_hi)
        return carry

    lax.fori_loop(0, n_blocks, body, 0)


def _attention_b(slopes, q, k, v, group):
    window, dilation = B_GROUPS[group]
    bsz, seq, w = q.shape
    n = seq // dilation
    rows = ATTN_TOKENS // dilation
    qb = rows // BLOCK
    width = dilation * w
    q, k, v = (t.reshape(bsz, n, width) for t in (q, k, v))
    kernel = functools.partial(
        _attn_b_kernel, n_blocks=qb, dilation=dilation, max_dist=window // dilation,
        head0=A_Q_HEADS + group * B_HEADS_PER_GROUP)
    cur = pl.BlockSpec((1, rows, width), lambda b, j: (b, j, 0))
    prev = pl.BlockSpec((1, BLOCK, width), lambda b, j: (b, jnp.maximum(j * qb - 1, 0), 0))
    o, lse = pl.pallas_call(
        kernel,
        grid=(bsz, n // rows),
        in_specs=[pl.BlockSpec(memory_space=pltpu.SMEM), cur, cur, prev, cur, prev],
        out_specs=[cur, cur],
        out_shape=[jax.ShapeDtypeStruct((bsz, n, width), BF16),
                   jax.ShapeDtypeStruct((bsz, n, width), F32)],
        scratch_shapes=[pltpu.VMEM((2, B_HEADS_PER_GROUP, BLOCK, 2 * BLOCK), F32),
                        pltpu.VMEM((rows + BLOCK, width), BF16),
                        pltpu.VMEM((rows + BLOCK, width), BF16)],
        compiler_params=pltpu.CompilerParams(
            dimension_semantics=("arbitrary", "arbitrary"), vmem_limit_bytes=VMEM_LIMIT),
        name=f"attn_dil{dilation}",
    )(slopes, q, k, k, v, v)
    return o.reshape(bsz * seq, w), lse.reshape(bsz * seq, w)


def _post_kernel(x_ref, mod_ref, ya_ref, o0_ref, o1_ref, o2_ref, l0_ref, l1_ref, l2_ref,
                 sa_ref, sb_ref, wa_ref, wb_ref, wo_ref, wg_ref, wu_ref, wd_ref, ln_ref,
                 out_ref, h_ref, *, alpha, ff_chunk):
    g1 = mod_ref[0, 2:3, :]
    sh2 = mod_ref[0, 3:4, :]
    s2 = mod_ref[0, 4:5, :]
    g2 = mod_ref[0, 5:6, :]

    l0, l1, l2 = l0_ref[...], l1_ref[...], l2_ref[...]
    lmax = jnp.maximum(jnp.maximum(l0, l1), l2)
    e0, e1, e2 = jnp.exp(l0 - lmax), jnp.exp(l1 - lmax), jnp.exp(l2 - lmax)
    yb = (e0 * o0_ref[...].astype(F32) + e1 * o1_ref[...].astype(F32)
          + e2 * o2_ref[...].astype(F32)) / (e0 + e1 + e2)

    a = jnp.dot(ya_ref[...], wa_ref[...], preferred_element_type=F32)
    bb = jnp.dot(yb.astype(BF16), wb_ref[...], preferred_element_type=F32)
    merged = sa_ref[...].astype(F32) * a + sb_ref[...].astype(F32) * bb
    t1 = jnp.dot(merged.astype(BF16), wo_ref[...], preferred_element_type=F32)
    x1 = _layer_norm(alpha * x_ref[...] + g1 * t1, ln_ref[0:1, :], ln_ref[1:2, :])

    u2 = (x1 * (1.0 + s2) + sh2).astype(BF16)
    d_ff = wg_ref.shape[1]
    for c in range(0, d_ff, ff_chunk):
        gate = jnp.dot(u2, wg_ref[:, c:c + ff_chunk], preferred_element_type=F32)
        up = jnp.dot(u2, wu_ref[:, c:c + ff_chunk], preferred_element_type=F32)
        h_ref[:, c:c + ff_chunk] = (gate * _sigmoid(gate) * up).astype(BF16)
    t2 = jnp.dot(h_ref[...], wd_ref[...], preferred_element_type=F32)
    out_ref[...] = _layer_norm(alpha * x1 + g2 * t2, ln_ref[2:3, :], ln_ref[3:4, :])


def _post(x2, mod_l, ya, obs, lses, sa, sb, wa, wb, wo, wg, wu, wd, ln, seq, alpha):
    t, d = x2.shape
    d_ff = wg.shape[1]
    tiles_per_seq = seq // ROW_TILE
    row = lambda w: pl.BlockSpec((ROW_TILE, w), lambda i: (i, 0))
    kernel = functools.partial(_post_kernel, alpha=alpha, ff_chunk=256)
    return pl.pallas_call(
        kernel,
        grid=(t // ROW_TILE,),
        in_specs=[row(d), pl.BlockSpec((1, 6, d), lambda i: (i // tiles_per_seq, 0, 0)),
                  row(A_Q_W)] + [row(B_GROUP_W)] * 6 + [row(d), row(d)] +
                 [_resident(w.shape) for w in (wa, wb, wo, wg, wu, wd, ln)],
        out_specs=row(d),
        out_shape=jax.ShapeDtypeStruct((t, d), F32),
        scratch_shapes=[pltpu.VMEM((ROW_TILE, d_ff), BF16)],
        compiler_params=pltpu.CompilerParams(
            dimension_semantics=("arbitrary",), vmem_limit_bytes=VMEM_LIMIT),
        name="merge_ffn",
    )(x2, mod_l, ya, *obs, *lses, sa, sb, wa, wb, wo, wg, wu, wd, ln)


def _pair_permutation():
    group = A_Q_HEADS // A_KV_HEADS
    cols = []
    for p in range(group):
        cols += list(range(p * HEAD_DIM, (p + 1) * HEAD_DIM))
        cols += list(range((p + group) * HEAD_DIM, (p + group + 1) * HEAD_DIM))
    return np.asarray(cols, np.int32)


def _prepare_w_in(w_in_l):
    d = w_in_l.shape[0]
    nb = len(B_GROUPS)
    b_w = nb * B_GROUP_W
    scale = HEAD_DIM ** -0.5
    o = 0
    qa = w_in_l[:, o:o + A_Q_W][:, _pair_permutation()] * scale; o += A_Q_W
    ka = w_in_l[:, o:o + A_KV_W]; o += A_KV_W
    va = w_in_l[:, o:o + A_KV_W]; o += A_KV_W
    qb = w_in_l[:, o:o + b_w] * scale; o += b_w
    kb = w_in_l[:, o:o + b_w]; o += b_w
    vb = w_in_l[:, o:o + b_w]; o += b_w
    gates = w_in_l[:, o:]
    parts = [qa, ka, va]
    for g in range(nb):
        sl = slice(g * B_GROUP_W, (g + 1) * B_GROUP_W)
        parts += [qb[:, sl], kb[:, sl], vb[:, sl]]
    parts.append(gates)
    return jnp.concatenate(parts, axis=1).astype(BF16)


def kernel(x, c, w_ada, b_ada, w_in, sinks, w_a, w_b, w_o, ln1_g, ln1_b,
           w_gate, w_up, w_down, ln2_g, ln2_b):
    bsz, seq, d = x.shape
    depth = w_ada.shape[0]
    alpha = (2 * depth) ** 0.25
    assert seq % (ATTN_TOKENS) == 0 and seq % ROW_TILE == 0

    slopes = jnp.exp2(-8.0 * jnp.arange(1, N_ATTN_HEADS + 1, dtype=F32) / N_ATTN_HEADS)
    mod = _modulation(c, w_ada, b_ada)
    perm = _pair_permutation()
    x2 = x.reshape(bsz * seq, d)
    for l in range(depth):
        outs = _inproj(x2, mod[l], _prepare_w_in(w_in[l]), seq)
        qa, kva = outs[0], outs[1]
        sa, sb = outs[-2], outs[-1]
        ya = _attention_a(slopes, sinks[l], qa.reshape(bsz, seq, -1), kva.reshape(bsz, seq, -1))
        obs, lses = [], []
        for g in range(len(B_GROUPS)):
            qg, kg, vg = (t.reshape(bsz, seq, -1) for t in outs[2 + 3 * g:5 + 3 * g])
            o, lse = _attention_b(slopes, qg, kg, vg, g)
            obs.append(o)
            lses.append(lse)
        ln = jnp.stack([ln1_g[l], ln1_b[l], ln2_g[l], ln2_b[l]])
        x2 = _post(x2, mod[l], ya.reshape(bsz * seq, -1), obs, lses, sa, sb,
                   w_a[l][perm].astype(BF16), w_b[l].astype(BF16), w_o[l].astype(BF16),
                   w_gate[l].astype(BF16), w_up[l].astype(BF16), w_down[l].astype(BF16),
                   ln, seq, alpha)
    return x2.reshape(bsz, seq, d)
```

```python
import functools
import math

import numpy as np
import jax
import jax.numpy as jnp
from jax import lax
from jax.experimental import pallas as pl
from jax.experimental.pallas import tpu as pltpu

HEAD_DIM = 64
A_Q_HEADS = 8
A_KV_HEADS = 2
A_WINDOW = 128
B_GROUPS = ((128, 1), (512, 4), (2048, 16))
B_HEADS_PER_GROUP = 4
N_ATTN_HEADS = A_Q_HEADS + B_HEADS_PER_GROUP * len(B_GROUPS)
BLOCK = 128
LN_EPS = 1e-5
NEG_INF = -1e30

LANES = 128
A_Q_W = A_Q_HEADS * HEAD_DIM
A_KV_W = A_KV_HEADS * HEAD_DIM
B_GROUP_W = B_HEADS_PER_GROUP * HEAD_DIM
SLABS = B_GROUP_W // LANES
IN_ROWS = 1024
ROW_TILE = 512
SUB_ROWS = 256
ATTN_ROWS = 1024
VMEM_LIMIT = 56 * 1024 * 1024
LOG2E = math.log2(math.e)

F32 = jnp.float32
BF16 = jnp.bfloat16


def _sigmoid(x):
    return 0.5 * jnp.tanh(0.5 * x) + 0.5


def _layer_norm(y, g, b):
    mu = jnp.mean(y, axis=-1, keepdims=True)
    yc = y - mu
    var = jnp.mean(yc * yc, axis=-1, keepdims=True)
    return yc * lax.rsqrt(var + LN_EPS) * g + b


def _resident(shape):
    return pl.BlockSpec(shape, lambda *_: (0,) * len(shape), pipeline_mode=pl.Buffered(1))


def _folded_spec(dilation, rows, tiles_per_seq):
    return pl.BlockSpec((1, dilation, rows // dilation, B_GROUP_W),
                        lambda i: (i // tiles_per_seq, 0, i % tiles_per_seq, 0))


def _mod_kernel(c_ref, w_ref, b_ref, o_ref):
    c = c_ref[...]
    sc = c * _sigmoid(c)
    o_ref[0] = jnp.dot(sc, w_ref[0], preferred_element_type=F32,
                       precision=lax.Precision.HIGHEST) + b_ref[0]


def _modulation(c, w_ada, b_ada):
    depth, d, n = w_ada.shape
    bsz = c.shape[0]
    rows = 8
    c_pad = jnp.pad(c, ((0, rows - bsz), (0, 0)))
    out = pl.pallas_call(
        _mod_kernel,
        grid=(depth, n // d),
        in_specs=[pl.BlockSpec((rows, d), lambda l, j: (0, 0)),
                  pl.BlockSpec((1, d, d), lambda l, j: (l, 0, j)),
                  pl.BlockSpec((1, 1, d), lambda l, j: (l, 0, j))],
        out_specs=pl.BlockSpec((1, rows, d), lambda l, j: (l, 0, j)),
        out_shape=jax.ShapeDtypeStruct((depth, rows, n), F32),
        compiler_params=pltpu.CompilerParams(
            dimension_semantics=("arbitrary", "arbitrary"), vmem_limit_bytes=VMEM_LIMIT),
        name="adaln_mod",
    )(c_pad, w_ada, b_ada.reshape(depth, 1, n))
    return out[:, :bsz].reshape(depth, bsz, n // d, d)


def _inproj_kernel(x_ref, mod_ref, w_ref, qa_ref, kva_ref, *rest):
    n_b = 3 * len(B_GROUPS)
    b_refs, (sa_ref, sb_ref, u_ref, fold_ref) = rest[:n_b], rest[n_b:]
    sh1 = mod_ref[0, 0:1, :]
    s1 = mod_ref[0, 1:2, :]
    u_ref[...] = (x_ref[...] * (1.0 + s1) + sh1).astype(BF16)

    def proj(off, width):
        return jnp.dot(u_ref[...], w_ref[:, off:off + width], preferred_element_type=F32)

    d_model = sa_ref.shape[1]
    gate_off = A_Q_W + 2 * A_KV_W + n_b * B_GROUP_W
    sa_ref[...] = _sigmoid(proj(gate_off, d_model)).astype(BF16)
    sb_ref[...] = _sigmoid(proj(gate_off + d_model, d_model)).astype(BF16)
    off = A_Q_W + 2 * A_KV_W
    slot = 0
    for g, (_, dilation) in enumerate(B_GROUPS):
        for t in range(3):
            r = proj(off, B_GROUP_W)
            ref = b_refs[3 * g + t]
            off += B_GROUP_W
            if dilation == 1:
                ref[0, 0] = r.astype(BF16)
                continue
            for s in range(SLABS):
                fold_ref[slot, s] = r[:, s * LANES:(s + 1) * LANES]
            for rr in range(dilation):
                for s in range(SLABS):
                    part = fold_ref[slot, s, pl.ds(rr, IN_ROWS // dilation, stride=dilation), :]
                    ref[0, rr, :, s * LANES:(s + 1) * LANES] = part.astype(BF16)
            slot += 1
    qa_ref[...] = proj(0, A_Q_W).astype(BF16)
    kva_ref[...] = proj(A_Q_W, 2 * A_KV_W).astype(BF16)


def _inproj(x2, mod_l, w_in_p, bsz, seq):
    t, d = x2.shape
    n = w_in_p.shape[1]
    tiles_per_seq = seq // IN_ROWS
    row = lambda w: pl.BlockSpec((IN_ROWS, w), lambda i: (i, 0))
    out_shapes = [jax.ShapeDtypeStruct((t, A_Q_W), BF16),
                  jax.ShapeDtypeStruct((t, 2 * A_KV_W), BF16)]
    out_specs = [row(A_Q_W), row(2 * A_KV_W)]
    n_folded = 0
    for _, dilation in B_GROUPS:
        for _ in range(3):
            out_shapes.append(jax.ShapeDtypeStruct((bsz, dilation, seq // dilation, B_GROUP_W), BF16))
            out_specs.append(_folded_spec(dilation, IN_ROWS, tiles_per_seq))
            n_folded += dilation > 1
    out_shapes += [jax.ShapeDtypeStruct((t, d), BF16)] * 2
    out_specs += [row(d), row(d)]
    return pl.pallas_call(
        _inproj_kernel,
        grid=(t // IN_ROWS,),
        in_specs=[row(d),
                  pl.BlockSpec((1, 6, d), lambda i: (i // tiles_per_seq, 0, 0)),
                  _resident((d, n))],
        out_specs=out_specs,
        out_shape=out_shapes,
        scratch_shapes=[pltpu.VMEM((IN_ROWS, d), BF16),
                        pltpu.VMEM((n_folded, SLABS, IN_ROWS, LANES), F32)],
        compiler_params=pltpu.CompilerParams(
            dimension_semantics=("arbitrary",), vmem_limit_bytes=VMEM_LIMIT),
        name="inproj",
    )(x2, mod_l, w_in_p)


def _fill_bias_tables(tab_ref, slopes_ref, head_ids, max_dist, stride, sinks_ref=None):
    qi = lax.broadcasted_iota(jnp.int32, (BLOCK, 2 * BLOCK), 0)
    sj = lax.broadcasted_iota(jnp.int32, (BLOCK, 2 * BLOCK), 1)
    dist = qi + BLOCK - sj
    valid = (dist >= 0) & (dist <= max_dist)
    valid_first = valid & (sj >= BLOCK)
    dist_f = (dist * stride).astype(F32)
    if sinks_ref is not None:
        assert max_dist < BLOCK
    for h, hid in enumerate(head_ids):
        bias = -(slopes_ref[hid] * dist_f) * LOG2E
        for v, ok in enumerate((valid, valid_first)):
            tab = jnp.where(ok, bias, NEG_INF)
            if sinks_ref is not None:
                tab = jnp.where(sj == 0, sinks_ref[hid] * LOG2E, tab)
            tab_ref[v, h * BLOCK:(h + 1) * BLOCK, :] = tab


def _attend(qs, k2, v2, tab):
    s = lax.dot_general(qs, k2, (((1,), (1,)), ((), ())), preferred_element_type=F32) + tab
    m = jnp.max(s, axis=-1, keepdims=True)
    e = jnp.exp2(s - m).astype(BF16)
    vext = jnp.concatenate([v2, jnp.ones((2 * BLOCK, LANES), BF16)], axis=1)
    acc = jnp.dot(e, vext, preferred_element_type=F32)
    return acc[:, :LANES], acc[:, LANES:], m


def _lane_masks():
    lane = lax.broadcasted_iota(jnp.int32, (1, LANES), 1)
    lo = lane < HEAD_DIM
    return lo, lo.astype(BF16), (~lo).astype(BF16)


def _prev_cur(prev_ref, cur_ref, i, cols):
    if i == 0:
        return jnp.concatenate([prev_ref[0, :, cols], cur_ref[0, 0:BLOCK, cols]], axis=0)
    return cur_ref[0, (i - 1) * BLOCK:(i + 1) * BLOCK, cols]


def _attn_a_kernel(slopes_ref, sinks_ref, q_ref, kvc_ref, kvp_ref, o_ref, tab_ref, *, n_blocks):
    b, j = pl.program_id(0), pl.program_id(1)
    group = A_Q_HEADS // A_KV_HEADS
    head_order = tuple(h for p in range(group) for h in (p, p + group))

    @pl.when((b == 0) & (j == 0))
    def _():
        _fill_bias_tables(tab_ref, slopes_ref, head_order, A_WINDOW - 1, 1, sinks_ref)

    lo, mask_lo, mask_hi = _lane_masks()
    first = jnp.where(j == 0, 1, 0)
    not_sink_row = lax.broadcasted_iota(jnp.int32, (2 * BLOCK, 1), 0) > 0
    for i in range(n_blocks):
        rows = slice(i * BLOCK, (i + 1) * BLOCK)
        kv = _prev_cur(kvp_ref, kvc_ref, i, slice(None))
        kv = jnp.where(not_sink_row, kv, jnp.zeros_like(kv))
        for p in range(group):
            cols = slice(p * LANES, (p + 1) * LANES)
            trows = slice(2 * p * BLOCK, (2 * p + 2) * BLOCK)
            q2 = q_ref[0, rows, cols]
            qs = jnp.concatenate([q2 * mask_lo, q2 * mask_hi], axis=0)
            tab = tab_ref[first, trows, :] if i == 0 else tab_ref[0, trows, :]
            acc, den, _ = _attend(qs, kv[:, :LANES], kv[:, LANES:], tab)
            o = acc / den
            o_ref[0, rows, cols] = jnp.where(lo, o[:BLOCK], o[BLOCK:]).astype(BF16)


def _attention_a(slopes, sinks_l, qa, kva):
    bsz, seq, _ = qa.shape
    qb = ATTN_ROWS // BLOCK
    kernel = functools.partial(_attn_a_kernel, n_blocks=qb)
    smem = pl.BlockSpec(memory_space=pltpu.SMEM)
    return pl.pallas_call(
        kernel,
        grid=(bsz, seq // ATTN_ROWS),
        in_specs=[smem, smem,
                  pl.BlockSpec((1, ATTN_ROWS, A_Q_W), lambda b, j: (b, j, 0)),
                  pl.BlockSpec((1, ATTN_ROWS, 2 * A_KV_W), lambda b, j: (b, j, 0)),
                  pl.BlockSpec((1, BLOCK, 2 * A_KV_W),
                               lambda b, j: (b, jnp.maximum(j * qb - 1, 0), 0))],
        out_specs=pl.BlockSpec((1, ATTN_ROWS, A_Q_W), lambda b, j: (b, j, 0)),
        out_shape=jax.ShapeDtypeStruct((bsz, seq, A_Q_W), BF16),
        scratch_shapes=[pltpu.VMEM((2, A_Q_HEADS * BLOCK, 2 * BLOCK), F32)],
        compiler_params=pltpu.CompilerParams(
            dimension_semantics=("arbitrary", "arbitrary"), vmem_limit_bytes=VMEM_LIMIT),
        name="attn_swa",
    )(slopes, sinks_l, qa, kva, kva)


def _attn_b_kernel(slopes_ref, q_ref, kc_ref, kp_ref, vc_ref, vp_ref, o_ref, lse_ref, tab_ref,
                   *, n_blocks, dilation, max_dist, head0):
    b, j = pl.program_id(0), pl.program_id(1)

    @pl.when((b == 0) & (j == 0))
    def _():
        heads = tuple(range(head0, head0 + B_HEADS_PER_GROUP))
        _fill_bias_tables(tab_ref, slopes_ref, heads, max_dist, dilation)

    lo, mask_lo, mask_hi = _lane_masks()
    first = jnp.where(j == 0, 1, 0)
    for i in range(n_blocks):
        rows = slice(i * BLOCK, (i + 1) * BLOCK)
        for p in range(B_HEADS_PER_GROUP // 2):
            cols = slice(p * LANES, (p + 1) * LANES)
            trows = slice(2 * p * BLOCK, (2 * p + 2) * BLOCK)
            q2 = q_ref[0, rows, cols]
            qs = jnp.concatenate([q2 * mask_lo, q2 * mask_hi], axis=0)
            tab = tab_ref[first, trows, :] if i == 0 else tab_ref[0, trows, :]
            acc, den, m = _attend(qs, _prev_cur(kp_ref, kc_ref, i, cols),
                                  _prev_cur(vp_ref, vc_ref, i, cols), tab)
            o = acc / den
            lse2 = m + jnp.log(den) * LOG2E
            o_ref[0, rows, cols] = jnp.where(lo, o[:BLOCK], o[BLOCK:]).astype(BF16)
            lse_ref[0, rows, cols] = jnp.where(lo, lse2[:BLOCK], lse2[BLOCK:])


def _attention_b(slopes, q, k, v, group):
    window, dilation = B_GROUPS[group]
    bsz, _, n, w = q.shape
    rows = min(ATTN_ROWS, n)
    qb = rows // BLOCK
    q, k, v = (t.reshape(bsz * dilation, n, w) for t in (q, k, v))
    kernel = functools.partial(
        _attn_b_kernel, n_blocks=qb, dilation=dilation, max_dist=window // dilation,
        head0=A_Q_HEADS + group * B_HEADS_PER_GROUP)
    cur = pl.BlockSpec((1, rows, w), lambda b, j: (b, j, 0))
    prev = pl.BlockSpec((1, BLOCK, w), lambda b, j: (b, jnp.maximum(j * qb - 1, 0), 0))
    o, lse = pl.pallas_call(
        kernel,
        grid=(bsz * dilation, n // rows),
        in_specs=[pl.BlockSpec(memory_space=pltpu.SMEM), cur, cur, prev, cur, prev],
        out_specs=[cur, cur],
        out_shape=[jax.ShapeDtypeStruct((bsz * dilation, n, w), BF16),
                   jax.ShapeDtypeStruct((bsz * dilation, n, w), F32)],
        scratch_shapes=[pltpu.VMEM((2, B_HEADS_PER_GROUP * BLOCK, 2 * BLOCK), F32)],
        compiler_params=pltpu.CompilerParams(
            dimension_semantics=("arbitrary", "arbitrary"), vmem_limit_bytes=VMEM_LIMIT),
        name=f"attn_dil{dilation}",
    )(slopes, q, k, k, v, v)
    return o.reshape(bsz, dilation, n, w), lse.reshape(bsz, dilation, n, w)


def _unfold(src_ref, slab_ref, slot, dilation, r0):
    if dilation == 1:
        return src_ref[0, 0, r0:r0 + SUB_ROWS, :].astype(F32)
    n = SUB_ROWS // dilation
    for rr in range(dilation):
        for s in range(SLABS):
            slab_ref[slot, s, pl.ds(r0 + rr, n, stride=dilation), :] = (
                src_ref[0, rr, r0 // dilation:r0 // dilation + n,
                        s * LANES:(s + 1) * LANES].astype(F32))
    return jnp.concatenate([slab_ref[slot, s, r0:r0 + SUB_ROWS, :] for s in range(SLABS)],
                           axis=1)


def _post_kernel(x_ref, mod_ref, ya_ref, o0_ref, o1_ref, o2_ref, l0_ref, l1_ref, l2_ref,
                 sa_ref, sb_ref, wa_ref, wb_ref, wo_ref, wg_ref, wu_ref, wd_ref, ln_ref,
                 out_ref, h_ref, slab_ref, x1_ref, act_ref, *, alpha, ff_chunk):
    g1 = mod_ref[0, 2:3, :]
    sh2 = mod_ref[0, 3:4, :]
    s2 = mod_ref[0, 4:5, :]
    g2 = mod_ref[0, 5:6, :]
    d_ff = wg_ref.shape[1]
    held = {}

    def mix(r0):
        slot = 0
        os_, ls_ = [], []
        for (_, dilation), o_ref, l_ref in zip(B_GROUPS, (o0_ref, o1_ref, o2_ref),
                                               (l0_ref, l1_ref, l2_ref)):
            os_.append(_unfold(o_ref, slab_ref, slot, dilation, r0))
            ls_.append(_unfold(l_ref, slab_ref, slot + 1, dilation, r0))
            slot += 2 * (dilation > 1)
        lmax = jnp.maximum(jnp.maximum(ls_[0], ls_[1]), ls_[2])
        es = [jnp.exp2(l - lmax) for l in ls_]
        yb = (es[0] * os_[0] + es[1] * os_[1] + es[2] * os_[2]) / (es[0] + es[1] + es[2])
        held[r0, "yb"] = yb.astype(BF16)

    def merge(r0):
        rs = slice(r0, r0 + SUB_ROWS)
        a = jnp.dot(ya_ref[rs, :], wa_ref[...], preferred_element_type=F32)
        bb = jnp.dot(held[r0, "yb"], wb_ref[...], preferred_element_type=F32)
        merged = sa_ref[rs, :].astype(F32) * a + sb_ref[rs, :].astype(F32) * bb
        act_ref[rs, :] = merged.astype(BF16)

    def out_proj(r0):
        rs = slice(r0, r0 + SUB_ROWS)
        held[r0, "t1"] = jnp.dot(act_ref[rs, :], wo_ref[...], preferred_element_type=F32)

    def norm1(r0):
        rs = slice(r0, r0 + SUB_ROWS)
        x1 = _layer_norm(alpha * x_ref[rs, :] + g1 * held[r0, "t1"],
                         ln_ref[0:1, :], ln_ref[1:2, :])
        x1_ref[rs, :] = x1
        act_ref[rs, :] = (x1 * (1.0 + s2) + sh2).astype(BF16)

    def ffn_up(r0, c):
        rs = slice(r0, r0 + SUB_ROWS)
        gate = jnp.dot(act_ref[rs, :], wg_ref[:, c:c + ff_chunk], preferred_element_type=F32)
        up = jnp.dot(act_ref[rs, :], wu_ref[:, c:c + ff_chunk], preferred_element_type=F32)
        h_ref[rs, c:c + ff_chunk] = (gate * _sigmoid(gate) * up).astype(BF16)

    def ffn_down(r0):
        rs = slice(r0, r0 + SUB_ROWS)
        held[r0, "t2"] = jnp.dot(h_ref[rs, :], wd_ref[...], preferred_element_type=F32)

    def norm2(r0):
        rs = slice(r0, r0 + SUB_ROWS)
        out_ref[rs, :] = _layer_norm(alpha * x1_ref[rs, :] + g2 * held[r0, "t2"],
                                     ln_ref[2:3, :], ln_ref[3:4, :])

    first, second = 0, SUB_ROWS
    chunks = list(range(0, d_ff, ff_chunk))
    mix(first); merge(first); mix(second); out_proj(first); merge(second)
    norm1(first); out_proj(second)
    for i, c in enumerate(chunks):
        ffn_up(first, c)
        if i == 2:
            norm1(second)
    ffn_down(first)
    for i, c in enumerate(chunks):
        ffn_up(second, c)
        if i == 2:
            norm2(first)
    ffn_down(second)
    norm2(second)


def _post(x2, mod_l, ya, obs, lses, sa, sb, wa, wb, wo, wg, wu, wd, ln, seq, alpha):
    t, d = x2.shape
    d_ff = wg.shape[1]
    tiles_per_seq = seq // ROW_TILE
    row = lambda w: pl.BlockSpec((ROW_TILE, w), lambda i: (i, 0))
    folded = [_folded_spec(dilation, ROW_TILE, tiles_per_seq) for _, dilation in B_GROUPS]
    n_slabs = 2 * sum(dilation > 1 for _, dilation in B_GROUPS)
    kernel = functools.partial(_post_kernel, alpha=alpha, ff_chunk=256)
    return pl.pallas_call(
        kernel,
        grid=(t // ROW_TILE,),
        in_specs=[row(d), pl.BlockSpec((1, 6, d), lambda i: (i // tiles_per_seq, 0, 0)),
                  row(A_Q_W)] + folded + folded + [row(d), row(d)] +
                 [_resident(w.shape) for w in (wa, wb, wo, wg, wu, wd, ln)],
        out_specs=row(d),
        out_shape=jax.ShapeDtypeStruct((t, d), F32),
        scratch_shapes=[pltpu.VMEM((ROW_TILE, d_ff), BF16),
                        pltpu.VMEM((n_slabs, SLABS, ROW_TILE, LANES), F32),
                        pltpu.VMEM((ROW_TILE, d), F32),
                        pltpu.VMEM((ROW_TILE, d), BF16)],
        compiler_params=pltpu.CompilerParams(
            dimension_semantics=("arbitrary",), vmem_limit_bytes=VMEM_LIMIT),
        name="merge_ffn",
    )(x2, mod_l, ya, *obs, *lses, sa, sb, wa, wb, wo, wg, wu, wd, ln)


def _pair_permutation():
    group = A_Q_HEADS // A_KV_HEADS
    cols = []
    for p in range(group):
        cols += list(range(p * HEAD_DIM, (p + 1) * HEAD_DIM))
        cols += list(range((p + group) * HEAD_DIM, (p + group + 1) * HEAD_DIM))
    return np.asarray(cols, np.int32)


def _prepare_w_in(w_in_l):
    nb = len(B_GROUPS)
    b_w = nb * B_GROUP_W
    scale = HEAD_DIM ** -0.5 * LOG2E
    o = 0
    qa = w_in_l[:, o:o + A_Q_W][:, _pair_permutation()] * scale; o += A_Q_W
    ka = w_in_l[:, o:o + A_KV_W]; o += A_KV_W
    va = w_in_l[:, o:o + A_KV_W]; o += A_KV_W
    qb = w_in_l[:, o:o + b_w] * scale; o += b_w
    kb = w_in_l[:, o:o + b_w]; o += b_w
    vb = w_in_l[:, o:o + b_w]; o += b_w
    gates = w_in_l[:, o:]
    parts = [qa, ka, va]
    for g in range(nb):
        sl = slice(g * B_GROUP_W, (g + 1) * B_GROUP_W)
        parts += [qb[:, sl], kb[:, sl], vb[:, sl]]
    parts.append(gates)
    return jnp.concatenate(parts, axis=1).astype(BF16)


def kernel(x, c, w_ada, b_ada, w_in, sinks, w_a, w_b, w_o, ln1_g, ln1_b,
           w_gate, w_up, w_down, ln2_g, ln2_b):
    bsz, seq, d = x.shape
    depth = w_ada.shape[0]
    alpha = (2 * depth) ** 0.25
    max_dilation = max(dil for _, dil in B_GROUPS)
    assert seq % ATTN_ROWS == 0 and seq % IN_ROWS == 0 and seq % ROW_TILE == 0
    assert seq % (max_dilation * BLOCK) == 0

    slopes = jnp.exp2(-8.0 * jnp.arange(1, N_ATTN_HEADS + 1, dtype=F32) / N_ATTN_HEADS)
    mod = _modulation(c, w_ada, b_ada)
    perm = _pair_permutation()
    x2 = x.reshape(bsz * seq, d)
    for l in range(depth):
        outs = _inproj(x2, mod[l], _prepare_w_in(w_in[l]), bsz, seq)
        qa, kva = outs[0], outs[1]
        sa, sb = outs[-2], outs[-1]
        ya = _attention_a(slopes, sinks[l], qa.reshape(bsz, seq, -1), kva.reshape(bsz, seq, -1))
        obs, lses = [], []
        for g in range(len(B_GROUPS)):
            o, lse = _attention_b(slopes, *outs[2 + 3 * g:5 + 3 * g], g)
            obs.append(o)
            lses.append(lse)
        ln = jnp.stack([ln1_g[l], ln1_b[l], ln2_g[l], ln2_b[l]])
        x2 = _post(x2, mod[l], ya.reshape(bsz * seq, -1), obs, lses, sa, sb,
                   w_a[l][perm].astype(BF16), w_b[l].astype(BF16), w_o[l].astype(BF16),
                   w_gate[l].astype(BF16), w_up[l].astype(BF16), w_down[l].astype(BF16),
                   ln, seq, alpha)
    return x2.reshape(bsz, seq, d)
```

```python
import functools
import math

import numpy as np
import jax
import jax.numpy as jnp
from jax import lax
from jax.experimental import pallas as pl
from jax.experimental.pallas import tpu as pltpu

HEAD_DIM = 64
A_Q_HEADS = 8
A_KV_HEADS = 2
A_WINDOW = 128
B_GROUPS = ((128, 1), (512, 4), (2048, 16))
B_HEADS_PER_GROUP = 4
N_ATTN_HEADS = A_Q_HEADS + B_HEADS_PER_GROUP * len(B_GROUPS)
BLOCK = 128
LN_EPS = 1e-5
NEG_INF = -1e30

LANES = 128
A_Q_W = A_Q_HEADS * HEAD_DIM
A_KV_W = A_KV_HEADS * HEAD_DIM
B_GROUP_W = B_HEADS_PER_GROUP * HEAD_DIM
SLABS = B_GROUP_W // LANES
IN_ROWS = 1024
ROW_TILE = 512
SUB_ROWS = 256
ATTN_ROWS = 4096
VMEM_LIMIT = 56 * 1024 * 1024
LOG2E = math.log2(math.e)
Q_SCALE = HEAD_DIM ** -0.5 * LOG2E

F32 = jnp.float32
BF16 = jnp.bfloat16


def _sigmoid(x):
    return 0.5 * jnp.tanh(0.5 * x) + 0.5


def _layer_norm(y, g, b):
    mu = jnp.mean(y, axis=-1, keepdims=True)
    yc = y - mu
    var = jnp.mean(yc * yc, axis=-1, keepdims=True)
    return yc * lax.rsqrt(var + LN_EPS) * g + b


def _resident(layer, shape):
    return pl.BlockSpec((None,) + tuple(shape), lambda *_: (layer,) + (0,) * len(shape),
                        pipeline_mode=pl.Buffered(1))


def _mod_spec(layer, tiles_per_seq, d):
    return pl.BlockSpec((None, 1, 6, d), lambda i: (layer, i // tiles_per_seq, 0, 0))


def _folded_spec(dilation, rows, tiles_per_seq):
    return pl.BlockSpec((1, dilation, rows // dilation, B_GROUP_W),
                        lambda i: (i // tiles_per_seq, 0, i % tiles_per_seq, 0))


def _mod_kernel(ct_ref, w_ref, b_ref, o_ref):
    w = w_ref[0]
    for b in range(o_ref.shape[1]):
        c = ct_ref[:, b:b + 1]
        sc = c * _sigmoid(c)
        o_ref[0, b:b + 1, :] = jnp.sum(sc * w, axis=0, keepdims=True) + b_ref[0]


def _modulation(c, w_ada, b_ada):
    depth, d, n = w_ada.shape
    bsz = c.shape[0]
    ct = jnp.pad(c.T, ((0, 0), (0, LANES - bsz)))
    out = pl.pallas_call(
        _mod_kernel,
        grid=(depth, n // d),
        in_specs=[pl.BlockSpec((d, LANES), lambda l, j: (0, 0)),
                  pl.BlockSpec((1, d, d), lambda l, j: (l, 0, j)),
                  pl.BlockSpec((1, 1, d), lambda l, j: (l, 0, j))],
        out_specs=pl.BlockSpec((1, bsz, d), lambda l, j: (l, 0, j)),
        out_shape=jax.ShapeDtypeStruct((depth, bsz, n), F32),
        compiler_params=pltpu.CompilerParams(
            dimension_semantics=("arbitrary", "arbitrary"), vmem_limit_bytes=VMEM_LIMIT),
        name="adaln_mod",
    )(ct, w_ada, b_ada.reshape(depth, 1, n))
    return out.reshape(depth, bsz, n // d, d)


def _inproj_kernel(x_ref, mod_ref, w_ref, wqa_ref, qa_ref, kva_ref, *rest):
    n_g = len(B_GROUPS)
    b_refs, (sa_ref, sb_ref, u_ref, fold_ref) = rest[:3 * n_g], rest[3 * n_g:]
    sh1 = mod_ref[0, 0:1, :]
    s1 = mod_ref[0, 1:2, :]
    u_ref[...] = (x_ref[...] * (1.0 + s1) + sh1).astype(BF16)

    def proj(off, width):
        return jnp.dot(u_ref[...], w_ref[:, off:off + width], preferred_element_type=F32)

    d_model = sa_ref.shape[1]
    kv_off = A_Q_W
    qkv_off = A_Q_W + 2 * A_KV_W
    gate_off = qkv_off + 3 * n_g * B_GROUP_W
    sa_ref[...] = _sigmoid(proj(gate_off, d_model)).astype(BF16)
    sb_ref[...] = _sigmoid(proj(gate_off + d_model, d_model)).astype(BF16)
    slot = 0
    for g, (_, dilation) in enumerate(B_GROUPS):
        for t in range(3):
            r = proj(qkv_off + (t * n_g + g) * B_GROUP_W, B_GROUP_W)
            if t == 0:
                r = r * Q_SCALE
            ref = b_refs[3 * g + t]
            if dilation == 1:
                ref[0, 0] = r.astype(BF16)
                continue
            for s in range(SLABS):
                fold_ref[slot, s] = r[:, s * LANES:(s + 1) * LANES]
            for rr in range(dilation):
                for s in range(SLABS):
                    part = fold_ref[slot, s, pl.ds(rr, IN_ROWS // dilation, stride=dilation), :]
                    ref[0, rr, :, s * LANES:(s + 1) * LANES] = part.astype(BF16)
            slot += 1
    qa = jnp.dot(u_ref[...], wqa_ref[...], preferred_element_type=F32)
    qa_ref[...] = (qa * Q_SCALE).astype(BF16)
    kva_ref[...] = proj(kv_off, 2 * A_KV_W).astype(BF16)


def _inproj(x2, mod, w_in_b, wqa_b, layer, bsz, seq):
    t, d = x2.shape
    n = w_in_b.shape[2]
    tiles_per_seq = seq // IN_ROWS
    row = lambda w: pl.BlockSpec((IN_ROWS, w), lambda i: (i, 0))
    out_shapes = [jax.ShapeDtypeStruct((t, A_Q_W), BF16),
                  jax.ShapeDtypeStruct((t, 2 * A_KV_W), BF16)]
    out_specs = [row(A_Q_W), row(2 * A_KV_W)]
    n_folded = 0
    for _, dilation in B_GROUPS:
        for _ in range(3):
            out_shapes.append(jax.ShapeDtypeStruct((bsz, dilation, seq // dilation, B_GROUP_W), BF16))
            out_specs.append(_folded_spec(dilation, IN_ROWS, tiles_per_seq))
            n_folded += dilation > 1
    out_shapes += [jax.ShapeDtypeStruct((t, d), BF16)] * 2
    out_specs += [row(d), row(d)]
    return pl.pallas_call(
        _inproj_kernel,
        grid=(t // IN_ROWS,),
        in_specs=[row(d), _mod_spec(layer, tiles_per_seq, d),
                  _resident(layer, (d, n)), _resident(layer, (d, A_Q_W))],
        out_specs=out_specs,
        out_shape=out_shapes,
        scratch_shapes=[pltpu.VMEM((IN_ROWS, d), BF16),
                        pltpu.VMEM((n_folded, SLABS, IN_ROWS, LANES), F32)],
        compiler_params=pltpu.CompilerParams(
            dimension_semantics=("arbitrary",), vmem_limit_bytes=VMEM_LIMIT),
        name="inproj",
    )(x2, mod, w_in_b, wqa_b)


def _fill_bias_tables(tab_ref, slopes_ref, head_ids, max_dist, stride, sinks_ref=None):
    qi = lax.broadcasted_iota(jnp.int32, (BLOCK, 2 * BLOCK), 0)
    sj = lax.broadcasted_iota(jnp.int32, (BLOCK, 2 * BLOCK), 1)
    dist = qi + BLOCK - sj
    valid = (dist >= 0) & (dist <= max_dist)
    valid_first = valid & (sj >= BLOCK)
    dist_f = (dist * stride).astype(F32)
    if sinks_ref is not None:
        assert max_dist < BLOCK
    for h, hid in enumerate(head_ids):
        bias = -(slopes_ref[hid] * dist_f) * LOG2E
        for v, ok in enumerate((valid, valid_first)):
            tab = jnp.where(ok, bias, NEG_INF)
            if sinks_ref is not None:
                tab = jnp.where(sj == 0, sinks_ref[hid] * LOG2E, tab)
            tab_ref[v, h * BLOCK:(h + 1) * BLOCK, :] = tab


def _attend(qs, k2, v2, tab):
    s = lax.dot_general(qs, k2, (((1,), (1,)), ((), ())), preferred_element_type=F32) + tab
    m = jnp.max(s, axis=-1, keepdims=True)
    e = jnp.exp2(s - m).astype(BF16)
    vext = jnp.concatenate([v2, jnp.ones((2 * BLOCK, LANES), BF16)], axis=1)
    acc = jnp.dot(e, vext, preferred_element_type=F32)
    return acc[:, :LANES], acc[:, LANES:], m


def _lane_masks():
    lane = lax.broadcasted_iota(jnp.int32, (1, LANES), 1)
    lo = lane < HEAD_DIM
    return lo, lo.astype(BF16), (~lo).astype(BF16)


def _prev_cur(prev_ref, cur_ref, i, cols, sq=0):
    if i == 0:
        return jnp.concatenate([prev_ref[sq, :, cols], cur_ref[sq, 0:BLOCK, cols]], axis=0)
    return cur_ref[sq, (i - 1) * BLOCK:(i + 1) * BLOCK, cols]


def _attn_a_kernel(slopes_ref, sinks_ref, q_ref, kvc_ref, kvp_ref, o_ref, tab_ref, *, n_blocks):
    b, j = pl.program_id(0), pl.program_id(1)
    group = A_Q_HEADS // A_KV_HEADS
    head_order = tuple(h for p in range(group) for h in (p, p + group))

    @pl.when((b == 0) & (j == 0))
    def _():
        _fill_bias_tables(tab_ref, slopes_ref, head_order, A_WINDOW - 1, 1, sinks_ref)

    lo, mask_lo, mask_hi = _lane_masks()
    first = jnp.where(j == 0, 1, 0)
    not_sink_row = lax.broadcasted_iota(jnp.int32, (2 * BLOCK, 1), 0) > 0
    for i in range(n_blocks):
        rows = slice(i * BLOCK, (i + 1) * BLOCK)
        kv = _prev_cur(kvp_ref, kvc_ref, i, slice(None))
        kv = jnp.where(not_sink_row, kv, jnp.zeros_like(kv))
        for p in range(group):
            cols = slice(p * LANES, (p + 1) * LANES)
            trows = slice(2 * p * BLOCK, (2 * p + 2) * BLOCK)
            q2 = q_ref[0, rows, cols]
            qs = jnp.concatenate([q2 * mask_lo, q2 * mask_hi], axis=0)
            tab = tab_ref[first, trows, :] if i == 0 else tab_ref[0, trows, :]
            acc, den, _ = _attend(qs, kv[:, :LANES], kv[:, LANES:], tab)
            acc, den = (jnp.where(lo, t[:BLOCK], t[BLOCK:]) for t in (acc, den))
            o_ref[0, rows, cols] = (acc / den).astype(BF16)


def _attention_a(slopes, sinks_l, qa, kva):
    bsz, seq, _ = qa.shape
    qb = ATTN_ROWS // BLOCK
    kernel = functools.partial(_attn_a_kernel, n_blocks=qb)
    smem = pl.BlockSpec(memory_space=pltpu.SMEM)
    return pl.pallas_call(
        kernel,
        grid=(bsz, seq // ATTN_ROWS),
        in_specs=[smem, smem,
                  pl.BlockSpec((1, ATTN_ROWS, A_Q_W), lambda b, j: (b, j, 0)),
                  pl.BlockSpec((1, ATTN_ROWS, 2 * A_KV_W), lambda b, j: (b, j, 0)),
                  pl.BlockSpec((1, BLOCK, 2 * A_KV_W),
                               lambda b, j: (b, jnp.maximum(j * qb - 1, 0), 0))],
        out_specs=pl.BlockSpec((1, ATTN_ROWS, A_Q_W), lambda b, j: (b, j, 0)),
        out_shape=jax.ShapeDtypeStruct((bsz, seq, A_Q_W), BF16),
        scratch_shapes=[pltpu.VMEM((2, A_Q_HEADS * BLOCK, 2 * BLOCK), F32)],
        compiler_params=pltpu.CompilerParams(
            dimension_semantics=("arbitrary", "arbitrary"), vmem_limit_bytes=VMEM_LIMIT),
        name="attn_swa",
    )(slopes, sinks_l, qa, kva, kva)


def _attn_b_kernel(slopes_ref, q_ref, kc_ref, kp_ref, vc_ref, vp_ref, o_ref, lse_ref, tab_ref,
                   *, n_seqs, n_blocks, dilation, max_dist, head0):
    b, j = pl.program_id(0), pl.program_id(1)

    @pl.when((b == 0) & (j == 0))
    def _():
        heads = tuple(range(head0, head0 + B_HEADS_PER_GROUP))
        _fill_bias_tables(tab_ref, slopes_ref, heads, max_dist, dilation)

    lo, mask_lo, mask_hi = _lane_masks()
    first = jnp.where(j == 0, 1, 0)
    for sq in range(n_seqs):
        for i in range(n_blocks):
            rows = slice(i * BLOCK, (i + 1) * BLOCK)
            for p in range(B_HEADS_PER_GROUP // 2):
                cols = slice(p * LANES, (p + 1) * LANES)
                trows = slice(2 * p * BLOCK, (2 * p + 2) * BLOCK)
                q2 = q_ref[sq, rows, cols]
                qs = jnp.concatenate([q2 * mask_lo, q2 * mask_hi], axis=0)
                tab = tab_ref[first, trows, :] if i == 0 else tab_ref[0, trows, :]
                acc, den, m = _attend(qs, _prev_cur(kp_ref, kc_ref, i, cols, sq),
                                      _prev_cur(vp_ref, vc_ref, i, cols, sq), tab)
                acc, den, m = (jnp.where(lo, t[:BLOCK], t[BLOCK:]) for t in (acc, den, m))
                o_ref[sq, rows, cols] = (acc / den).astype(BF16)
                lse_ref[sq, rows, cols] = m + jnp.log(den) * LOG2E


def _attention_b(slopes, q, k, v, group):
    window, dilation = B_GROUPS[group]
    bsz, _, n, w = q.shape
    rows = min(ATTN_ROWS, n)
    qb = rows // BLOCK
    n_seqs = ATTN_ROWS // rows
    assert (bsz * dilation) % n_seqs == 0
    q, k, v = (t.reshape(bsz * dilation, n, w) for t in (q, k, v))
    kernel = functools.partial(
        _attn_b_kernel, n_seqs=n_seqs, n_blocks=qb, dilation=dilation,
        max_dist=window // dilation, head0=A_Q_HEADS + group * B_HEADS_PER_GROUP)
    cur = pl.BlockSpec((n_seqs, rows, w), lambda b, j: (b, j, 0))
    prev = pl.BlockSpec((n_seqs, BLOCK, w), lambda b, j: (b, jnp.maximum(j * qb - 1, 0), 0))
    o, lse = pl.pallas_call(
        kernel,
        grid=(bsz * dilation // n_seqs, n // rows),
        in_specs=[pl.BlockSpec(memory_space=pltpu.SMEM), cur, cur, prev, cur, prev],
        out_specs=[cur, cur],
        out_shape=[jax.ShapeDtypeStruct((bsz * dilation, n, w), BF16),
                   jax.ShapeDtypeStruct((bsz * dilation, n, w), F32)],
        scratch_shapes=[pltpu.VMEM((2, B_HEADS_PER_GROUP * BLOCK, 2 * BLOCK), F32)],
        compiler_params=pltpu.CompilerParams(
            dimension_semantics=("arbitrary", "arbitrary"), vmem_limit_bytes=VMEM_LIMIT),
        name=f"attn_dil{dilation}",
    )(slopes, q, k, k, v, v)
    return o.reshape(bsz, dilation, n, w), lse.reshape(bsz, dilation, n, w)


def _unfold(src_ref, slab_ref, slot, dilation, r0):
    if dilation == 1:
        return src_ref[0, 0, r0:r0 + SUB_ROWS, :].astype(F32)
    n = SUB_ROWS // dilation
    for rr in range(dilation):
        for s in range(SLABS):
            slab_ref[slot, s, pl.ds(r0 + rr, n, stride=dilation), :] = (
                src_ref[0, rr, r0 // dilation:r0 // dilation + n,
                        s * LANES:(s + 1) * LANES].astype(F32))
    return jnp.concatenate([slab_ref[slot, s, r0:r0 + SUB_ROWS, :] for s in range(SLABS)],
                           axis=1)


def _post_kernel(x_ref, mod_ref, ya_ref, o0_ref, o1_ref, o2_ref, l0_ref, l1_ref, l2_ref,
                 sa_ref, sb_ref, wa_ref, wb_ref, wo_ref, wg_ref, wu_ref, wd_ref, ln_ref,
                 out_ref, h_ref, slab_ref, x1_ref, act_ref, *, alpha, ff_chunk):
    g1 = mod_ref[0, 2:3, :]
    sh2 = mod_ref[0, 3:4, :]
    s2 = mod_ref[0, 4:5, :]
    g2 = mod_ref[0, 5:6, :]
    d_ff = wg_ref.shape[1]
    held = {}

    def mix(r0):
        slot = 0
        os_, ls_ = [], []
        for (_, dilation), o_ref, l_ref in zip(B_GROUPS, (o0_ref, o1_ref, o2_ref),
                                               (l0_ref, l1_ref, l2_ref)):
            os_.append(_unfold(o_ref, slab_ref, slot, dilation, r0))
            ls_.append(_unfold(l_ref, slab_ref, slot + 1, dilation, r0))
            slot += 2 * (dilation > 1)
        lmax = jnp.maximum(jnp.maximum(ls_[0], ls_[1]), ls_[2])
        es = [jnp.exp2(l - lmax) for l in ls_]
        yb = (es[0] * os_[0] + es[1] * os_[1] + es[2] * os_[2]) / (es[0] + es[1] + es[2])
        held[r0, "yb"] = yb.astype(BF16)

    def merge(r0):
        rs = slice(r0, r0 + SUB_ROWS)
        a = jnp.dot(ya_ref[rs, :], wa_ref[...], preferred_element_type=F32)
        bb = jnp.dot(held[r0, "yb"], wb_ref[...], preferred_element_type=F32)
        merged = sa_ref[rs, :].astype(F32) * a + sb_ref[rs, :].astype(F32) * bb
        act_ref[rs, :] = merged.astype(BF16)

    def out_proj(r0):
        rs = slice(r0, r0 + SUB_ROWS)
        held[r0, "t1"] = jnp.dot(act_ref[rs, :], wo_ref[...], preferred_element_type=F32)

    def norm1(r0):
        rs = slice(r0, r0 + SUB_ROWS)
        x1 = _layer_norm(alpha * x_ref[rs, :] + g1 * held[r0, "t1"],
                         ln_ref[0:1, :], ln_ref[1:2, :])
        x1_ref[rs, :] = x1
        act_ref[rs, :] = (x1 * (1.0 + s2) + sh2).astype(BF16)

    def ffn_up(r0, c):
        rs = slice(r0, r0 + SUB_ROWS)
        gate = jnp.dot(act_ref[rs, :], wg_ref[:, c:c + ff_chunk], preferred_element_type=F32)
        up = jnp.dot(act_ref[rs, :], wu_ref[:, c:c + ff_chunk], preferred_element_type=F32)
        h_ref[rs, c:c + ff_chunk] = (gate * _sigmoid(gate) * up).astype(BF16)

    def ffn_down(r0):
        rs = slice(r0, r0 + SUB_ROWS)
        held[r0, "t2"] = jnp.dot(h_ref[rs, :], wd_ref[...], preferred_element_type=F32)

    def norm2(r0):
        rs = slice(r0, r0 + SUB_ROWS)
        out_ref[rs, :] = _layer_norm(alpha * x1_ref[rs, :] + g2 * held[r0, "t2"],
                                     ln_ref[2:3, :], ln_ref[3:4, :])

    first, second = 0, SUB_ROWS
    chunks = list(range(0, d_ff, ff_chunk))
    mix(first); merge(first); mix(second); out_proj(first); merge(second)
    norm1(first); out_proj(second)
    for i, c in enumerate(chunks):
        ffn_up(first, c)
        if i == 2:
            norm1(second)
    ffn_down(first)
    for i, c in enumerate(chunks):
        ffn_up(second, c)
        if i == 2:
            norm2(first)
    ffn_down(second)
    norm2(second)


def _post(x2, mod, ya, obs, lses, sa, sb, weights, layer, seq, alpha):
    t, d = x2.shape
    d_ff = weights[3].shape[2]
    tiles_per_seq = seq // ROW_TILE
    row = lambda w: pl.BlockSpec((ROW_TILE, w), lambda i: (i, 0))
    folded = [_folded_spec(dilation, ROW_TILE, tiles_per_seq) for _, dilation in B_GROUPS]
    n_slabs = 2 * sum(dilation > 1 for _, dilation in B_GROUPS)
    kernel = functools.partial(_post_kernel, alpha=alpha, ff_chunk=256)
    return pl.pallas_call(
        kernel,
        grid=(t // ROW_TILE,),
        in_specs=[row(d), _mod_spec(layer, tiles_per_seq, d),
                  row(A_Q_W)] + folded + folded + [row(d), row(d)] +
                 [_resident(layer, w.shape[1:]) for w in weights],
        out_specs=row(d),
        out_shape=jax.ShapeDtypeStruct((t, d), F32),
        scratch_shapes=[pltpu.VMEM((ROW_TILE, d_ff), BF16),
                        pltpu.VMEM((n_slabs, SLABS, ROW_TILE, LANES), F32),
                        pltpu.VMEM((ROW_TILE, d), F32),
                        pltpu.VMEM((ROW_TILE, d), BF16)],
        compiler_params=pltpu.CompilerParams(
            dimension_semantics=("arbitrary",), vmem_limit_bytes=VMEM_LIMIT),
        name="merge_ffn",
    )(x2, mod, ya, *obs, *lses, sa, sb, *weights)


def _pair_permutation():
    group = A_Q_HEADS // A_KV_HEADS
    cols = []
    for p in range(group):
        cols += list(range(p * HEAD_DIM, (p + 1) * HEAD_DIM))
        cols += list(range((p + group) * HEAD_DIM, (p + group + 1) * HEAD_DIM))
    return np.asarray(cols, np.int32)


def kernel(x, c, w_ada, b_ada, w_in, sinks, w_a, w_b, w_o, ln1_g, ln1_b,
           w_gate, w_up, w_down, ln2_g, ln2_b):
    bsz, seq, d = x.shape
    depth = w_ada.shape[0]
    alpha = (2 * depth) ** 0.25
    max_dilation = max(dil for _, dil in B_GROUPS)
    assert seq % ATTN_ROWS == 0 and seq % IN_ROWS == 0 and seq % ROW_TILE == 0
    assert seq % (max_dilation * BLOCK) == 0

    slopes = jnp.exp2(-8.0 * jnp.arange(1, N_ATTN_HEADS + 1, dtype=F32) / N_ATTN_HEADS)
    mod = _modulation(c, w_ada, b_ada)
    perm = _pair_permutation()
    w_in_b = w_in.astype(BF16)
    wqa_b = w_in[:, :, :A_Q_W][:, :, perm].astype(BF16)
    post_weights = (w_a[:, perm, :].astype(BF16), w_b.astype(BF16), w_o.astype(BF16),
                    w_gate.astype(BF16), w_up.astype(BF16), w_down.astype(BF16),
                    jnp.stack([ln1_g, ln1_b, ln2_g, ln2_b], axis=1))
    x2 = x.reshape(bsz * seq, d)
    for l in range(depth):
        outs = _inproj(x2, mod, w_in_b, wqa_b, l, bsz, seq)
        qa, kva = outs[0], outs[1]
        sa, sb = outs[-2], outs[-1]
        ya = _attention_a(slopes, sinks[l], qa.reshape(bsz, seq, -1), kva.reshape(bsz, seq, -1))
        obs, lses = [], []
        for g in range(len(B_GROUPS)):
            o, lse = _attention_b(slopes, *outs[2 + 3 * g:5 + 3 * g], g)
            obs.append(o)
            lses.append(lse)
        x2 = _post(x2, mod, ya.reshape(bsz * seq, -1), obs, lses, sa, sb, post_weights, l,
                   seq, alpha)
    return x2.reshape(bsz, seq, d)
```

```python
import functools
import math

import jax
import jax.numpy as jnp
from jax import lax
from jax.experimental import pallas as pl
from jax.experimental.pallas import tpu as pltpu

HEAD_DIM = 64
A_Q_HEADS = 8
A_KV_HEADS = 2
A_WINDOW = 128
B_GROUPS = ((128, 1), (512, 4), (2048, 16))
B_HEADS_PER_GROUP = 4
N_ATTN_HEADS = A_Q_HEADS + B_HEADS_PER_GROUP * len(B_GROUPS)
BLOCK = 128
LN_EPS = 1e-5
NEG_INF = -1e30

LANES = 128
A_Q_W = A_Q_HEADS * HEAD_DIM
A_KV_W = A_KV_HEADS * HEAD_DIM
B_GROUP_W = B_HEADS_PER_GROUP * HEAD_DIM
SLABS = B_GROUP_W // LANES
FOLD_STRIDE = 4
IN_ROWS = 1024
ROW_TILE = 512
SUB_ROWS = 256
ATTN_ROWS = 4096
VMEM_LIMIT = 56 * 1024 * 1024
LOG2E = math.log2(math.e)
Q_SCALE = HEAD_DIM ** -0.5 * LOG2E

F32 = jnp.float32
BF16 = jnp.bfloat16


def _sigmoid(x):
    return 0.5 * jnp.tanh(0.5 * x) + 0.5


def _layer_norm(y, g, b):
    mu = jnp.mean(y, axis=-1, keepdims=True)
    yc = y - mu
    var = jnp.mean(yc * yc, axis=-1, keepdims=True)
    return yc * lax.rsqrt(var + LN_EPS) * g + b


def _resident(layer, shape):
    return pl.BlockSpec((None,) + tuple(shape), lambda *_: (layer,) + (0,) * len(shape),
                        pipeline_mode=pl.Buffered(1))


def _mod_spec(layer, tiles_per_seq, d):
    return pl.BlockSpec((None, 1, 6, d), lambda i: (layer, i // tiles_per_seq, 0, 0))


def _folded_spec(dilation, rows, tiles_per_seq):
    return pl.BlockSpec((1, dilation, rows // dilation, B_GROUP_W),
                        lambda i: (i // tiles_per_seq, 0, i % tiles_per_seq, 0))


def _mod_kernel(ct_ref, w_ref, b_ref, o_ref):
    w = w_ref[0]
    for b in range(o_ref.shape[1]):
        c = ct_ref[:, b:b + 1]
        sc = c * _sigmoid(c)
        o_ref[0, b:b + 1, :] = jnp.sum(sc * w, axis=0, keepdims=True) + b_ref[0]


def _modulation(c, w_ada, b_ada):
    depth, d, n = w_ada.shape
    bsz = c.shape[0]
    ct = jnp.pad(c.T, ((0, 0), (0, LANES - bsz)))
    out = pl.pallas_call(
        _mod_kernel,
        grid=(depth, n // d),
        in_specs=[pl.BlockSpec((d, LANES), lambda l, j: (0, 0)),
                  pl.BlockSpec((1, d, d), lambda l, j: (l, 0, j)),
                  pl.BlockSpec((1, 1, d), lambda l, j: (l, 0, j))],
        out_specs=pl.BlockSpec((1, bsz, d), lambda l, j: (l, 0, j)),
        out_shape=jax.ShapeDtypeStruct((depth, bsz, n), F32),
        compiler_params=pltpu.CompilerParams(
            dimension_semantics=("arbitrary", "arbitrary"), vmem_limit_bytes=VMEM_LIMIT),
        name="adaln_mod",
    )(ct, w_ada, b_ada.reshape(depth, 1, n))
    return out.reshape(depth, bsz, n // d, d)


def _inproj_kernel(x_ref, mod_ref, w_ref, wqa_ref, qa_ref, kva_ref, *rest):
    n_g = len(B_GROUPS)
    b_refs, (sa_ref, sb_ref, u_ref, fold_ref, tmp_ref) = rest[:3 * n_g], rest[3 * n_g:]
    sh1 = mod_ref[0, 0:1, :]
    s1 = mod_ref[0, 1:2, :]
    u_ref[...] = (x_ref[...] * (1.0 + s1) + sh1).astype(BF16)

    def proj(off, width):
        return jnp.dot(u_ref[...], w_ref[:, off:off + width], preferred_element_type=F32)

    d_model = sa_ref.shape[1]
    kv_off = A_Q_W
    qkv_off = A_Q_W + 2 * A_KV_W
    gate_off = qkv_off + 3 * n_g * B_GROUP_W
    sa_ref[...] = _sigmoid(proj(gate_off, d_model)).astype(BF16)
    sb_ref[...] = _sigmoid(proj(gate_off + d_model, d_model)).astype(BF16)
    slot = slot2 = 0
    for g, (_, dilation) in enumerate(B_GROUPS):
        for t in range(3):
            r = proj(qkv_off + (t * n_g + g) * B_GROUP_W, B_GROUP_W)
            if t == 0:
                r = r * Q_SCALE
            ref = b_refs[3 * g + t]
            if dilation == 1:
                ref[0, 0] = r.astype(BF16)
                continue
            for s in range(SLABS):
                fold_ref[slot, s] = r[:, s * LANES:(s + 1) * LANES]
            if dilation == FOLD_STRIDE:
                for rr in range(dilation):
                    for s in range(SLABS):
                        part = fold_ref[slot, s, pl.ds(rr, IN_ROWS // dilation, stride=dilation), :]
                        ref[0, rr, :, s * LANES:(s + 1) * LANES] = part.astype(BF16)
            else:
                assert dilation == FOLD_STRIDE * FOLD_STRIDE
                for s in range(SLABS):
                    for r1 in range(FOLD_STRIDE):
                        tmp_ref[slot2, s, r1] = fold_ref[
                            slot, s, pl.ds(r1, IN_ROWS // FOLD_STRIDE, stride=FOLD_STRIDE), :]
                for r1 in range(FOLD_STRIDE):
                    for r2 in range(FOLD_STRIDE):
                        for s in range(SLABS):
                            part = tmp_ref[slot2, s, r1,
                                           pl.ds(r2, IN_ROWS // dilation, stride=FOLD_STRIDE), :]
                            ref[0, r1 + FOLD_STRIDE * r2, :, s * LANES:(s + 1) * LANES] = (
                                part.astype(BF16))
                slot2 += 1
            slot += 1
    qa = jnp.dot(u_ref[...], wqa_ref[...], preferred_element_type=F32)
    qa_ref[...] = (qa * Q_SCALE).astype(BF16)
    kva_ref[...] = proj(kv_off, 2 * A_KV_W).astype(BF16)


def _inproj(x2, mod, w_in_b, wqa_b, layer, bsz, seq):
    t, d = x2.shape
    n = w_in_b.shape[2]
    tiles_per_seq = seq // IN_ROWS
    row = lambda w: pl.BlockSpec((IN_ROWS, w), lambda i: (i, 0))
    out_shapes = [jax.ShapeDtypeStruct((t, A_Q_W), BF16),
                  jax.ShapeDtypeStruct((t, 2 * A_KV_W), BF16)]
    out_specs = [row(A_Q_W), row(2 * A_KV_W)]
    n_folded = n_two_pass = 0
    for _, dilation in B_GROUPS:
        for _ in range(3):
            out_shapes.append(jax.ShapeDtypeStruct((bsz, dilation, seq // dilation, B_GROUP_W), BF16))
            out_specs.append(_folded_spec(dilation, IN_ROWS, tiles_per_seq))
            n_folded += dilation > 1
            n_two_pass += dilation > FOLD_STRIDE
    out_shapes += [jax.ShapeDtypeStruct((t, d), BF16)] * 2
    out_specs += [row(d), row(d)]
    return pl.pallas_call(
        _inproj_kernel,
        grid=(t // IN_ROWS,),
        in_specs=[row(d), _mod_spec(layer, tiles_per_seq, d),
                  _resident(layer, (d, n)), _resident(layer, (d, A_Q_W))],
        out_specs=out_specs,
        out_shape=out_shapes,
        scratch_shapes=[pltpu.VMEM((IN_ROWS, d), BF16),
                        pltpu.VMEM((n_folded, SLABS, IN_ROWS, LANES), F32),
                        pltpu.VMEM((n_two_pass, SLABS, FOLD_STRIDE, IN_ROWS // FOLD_STRIDE, LANES),
                                   F32)],
        compiler_params=pltpu.CompilerParams(
            dimension_semantics=("arbitrary",), vmem_limit_bytes=VMEM_LIMIT),
        name="inproj",
    )(x2, mod, w_in_b, wqa_b)


def _fill_bias_tables(tab_ref, slopes_ref, head_ids, max_dist, stride, sinks_ref=None):
    qi = lax.broadcasted_iota(jnp.int32, (BLOCK, 2 * BLOCK), 0)
    sj = lax.broadcasted_iota(jnp.int32, (BLOCK, 2 * BLOCK), 1)
    dist = qi + BLOCK - sj
    valid = (dist >= 0) & (dist <= max_dist)
    valid_first = valid & (sj >= BLOCK)
    dist_f = (dist * stride).astype(F32)
    if sinks_ref is not None:
        assert max_dist < BLOCK
    for h, hid in enumerate(head_ids):
        bias = -(slopes_ref[hid] * dist_f) * LOG2E
        for v, ok in enumerate((valid, valid_first)):
            tab = jnp.where(ok, bias, NEG_INF)
            if sinks_ref is not None:
                tab = jnp.where(sj == 0, sinks_ref[hid] * LOG2E, tab)
            tab_ref[v, h * BLOCK:(h + 1) * BLOCK, :] = tab


def _attend(qs, k2, v2, tab):
    s = lax.dot_general(qs, k2, (((1,), (1,)), ((), ())), preferred_element_type=F32) + tab
    m = jnp.max(s, axis=-1, keepdims=True)
    e = jnp.exp2(s - m).astype(BF16)
    vext = jnp.concatenate([v2, jnp.ones((2 * BLOCK, LANES), BF16)], axis=1)
    acc = jnp.dot(e, vext, preferred_element_type=F32)
    return acc[:, :LANES], acc[:, LANES:], m


def _lane_masks():
    lane = lax.broadcasted_iota(jnp.int32, (1, LANES), 1)
    lo = lane < HEAD_DIM
    return lo, lo.astype(BF16), (~lo).astype(BF16)


def _prev_cur(prev_ref, cur_ref, i, cols, sq=0):
    if i == 0:
        return jnp.concatenate([prev_ref[sq, :, cols], cur_ref[sq, 0:BLOCK, cols]], axis=0)
    return cur_ref[sq, (i - 1) * BLOCK:(i + 1) * BLOCK, cols]


def _attn_a_kernel(slopes_ref, sinks_ref, q_ref, kvc_ref, kvp_ref, o_ref, tab_ref, *, n_blocks):
    b, j = pl.program_id(0), pl.program_id(1)
    group = A_Q_HEADS // A_KV_HEADS
    head_order = tuple(h for p in range(group) for h in (p, p + group))

    @pl.when((b == 0) & (j == 0))
    def _():
        _fill_bias_tables(tab_ref, slopes_ref, head_order, A_WINDOW - 1, 1, sinks_ref)

    lo, mask_lo, mask_hi = _lane_masks()
    first = jnp.where(j == 0, 1, 0)
    not_sink_row = lax.broadcasted_iota(jnp.int32, (2 * BLOCK, 1), 0) > 0
    for i in range(n_blocks):
        rows = slice(i * BLOCK, (i + 1) * BLOCK)
        kv = _prev_cur(kvp_ref, kvc_ref, i, slice(None))
        kv = jnp.where(not_sink_row, kv, jnp.zeros_like(kv))
        for p in range(group):
            cols = slice(p * LANES, (p + 1) * LANES)
            trows = slice(2 * p * BLOCK, (2 * p + 2) * BLOCK)
            q2 = q_ref[0, rows, cols]
            qs = jnp.concatenate([q2 * mask_lo, q2 * mask_hi], axis=0)
            tab = tab_ref[first, trows, :] if i == 0 else tab_ref[0, trows, :]
            acc, den, _ = _attend(qs, kv[:, :LANES], kv[:, LANES:], tab)
            acc, den = (jnp.where(lo, t[:BLOCK], t[BLOCK:]) for t in (acc, den))
            o_ref[0, rows, cols] = (acc / den).astype(BF16)


def _attention_a(slopes, sinks_l, qa, kva):
    bsz, seq, _ = qa.shape
    qb = ATTN_ROWS // BLOCK
    kernel = functools.partial(_attn_a_kernel, n_blocks=qb)
    smem = pl.BlockSpec(memory_space=pltpu.SMEM)
    return pl.pallas_call(
        kernel,
        grid=(bsz, seq // ATTN_ROWS),
        in_specs=[smem, smem,
                  pl.BlockSpec((1, ATTN_ROWS, A_Q_W), lambda b, j: (b, j, 0)),
                  pl.BlockSpec((1, ATTN_ROWS, 2 * A_KV_W), lambda b, j: (b, j, 0)),
                  pl.BlockSpec((1, BLOCK, 2 * A_KV_W),
                               lambda b, j: (b, jnp.maximum(j * qb - 1, 0), 0))],
        out_specs=pl.BlockSpec((1, ATTN_ROWS, A_Q_W), lambda b, j: (b, j, 0)),
        out_shape=jax.ShapeDtypeStruct((bsz, seq, A_Q_W), BF16),
        scratch_shapes=[pltpu.VMEM((2, A_Q_HEADS * BLOCK, 2 * BLOCK), F32)],
        compiler_params=pltpu.CompilerParams(
            dimension_semantics=("arbitrary", "arbitrary"), vmem_limit_bytes=VMEM_LIMIT),
        name="attn_swa",
    )(slopes, sinks_l, qa, kva, kva)


def _attn_b_kernel(slopes_ref, q_ref, kc_ref, kp_ref, vc_ref, vp_ref, o_ref, lse_ref, tab_ref,
                   *, n_seqs, n_blocks, dilation, max_dist, head0):
    b, j = pl.program_id(0), pl.program_id(1)

    @pl.when((b == 0) & (j == 0))
    def _():
        heads = tuple(range(head0, head0 + B_HEADS_PER_GROUP))
        _fill_bias_tables(tab_ref, slopes_ref, heads, max_dist, dilation)

    lo, mask_lo, mask_hi = _lane_masks()
    first = jnp.where(j == 0, 1, 0)
    for sq in range(n_seqs):
        for i in range(n_blocks):
            rows = slice(i * BLOCK, (i + 1) * BLOCK)
            for p in range(B_HEADS_PER_GROUP // 2):
                cols = slice(p * LANES, (p + 1) * LANES)
                trows = slice(2 * p * BLOCK, (2 * p + 2) * BLOCK)
                q2 = q_ref[sq, rows, cols]
                qs = jnp.concatenate([q2 * mask_lo, q2 * mask_hi], axis=0)
                tab = tab_ref[first, trows, :] if i == 0 else tab_ref[0, trows, :]
                acc, den, m = _attend(qs, _prev_cur(kp_ref, kc_ref, i, cols, sq),
                                      _prev_cur(vp_ref, vc_ref, i, cols, sq), tab)
                acc, den, m = (jnp.where(lo, t[:BLOCK], t[BLOCK:]) for t in (acc, den, m))
                o_ref[sq, rows, cols] = (acc / den).astype(BF16)
                lse_ref[sq, rows, cols] = m + jnp.log(den) * LOG2E


def _attention_b(slopes, q, k, v, group):
    window, dilation = B_GROUPS[group]
    bsz, _, n, w = q.shape
    rows = min(ATTN_ROWS, n)
    qb = rows // BLOCK
    n_seqs = ATTN_ROWS // rows
    assert (bsz * dilation) % n_seqs == 0
    q, k, v = (t.reshape(bsz * dilation, n, w) for t in (q, k, v))
    kernel = functools.partial(
        _attn_b_kernel, n_seqs=n_seqs, n_blocks=qb, dilation=dilation,
        max_dist=window // dilation, head0=A_Q_HEADS + group * B_HEADS_PER_GROUP)
    cur = pl.BlockSpec((n_seqs, rows, w), lambda b, j: (b, j, 0))
    prev = pl.BlockSpec((n_seqs, BLOCK, w), lambda b, j: (b, jnp.maximum(j * qb - 1, 0), 0))
    o, lse = pl.pallas_call(
        kernel,
        grid=(bsz * dilation // n_seqs, n // rows),
        in_specs=[pl.BlockSpec(memory_space=pltpu.SMEM), cur, cur, prev, cur, prev],
        out_specs=[cur, cur],
        out_shape=[jax.ShapeDtypeStruct((bsz * dilation, n, w), BF16),
                   jax.ShapeDtypeStruct((bsz * dilation, n, w), F32)],
        scratch_shapes=[pltpu.VMEM((2, B_HEADS_PER_GROUP * BLOCK, 2 * BLOCK), F32)],
        compiler_params=pltpu.CompilerParams(
            dimension_semantics=("arbitrary", "arbitrary"), vmem_limit_bytes=VMEM_LIMIT),
        name=f"attn_dil{dilation}",
    )(slopes, q, k, k, v, v)
    return o.reshape(bsz, dilation, n, w), lse.reshape(bsz, dilation, n, w)


def _unfold(src_ref, slab_ref, tmp_ref, slot, dilation, r0):
    if dilation == 1:
        return src_ref[0, 0, r0:r0 + SUB_ROWS, :].astype(F32)
    n = SUB_ROWS // dilation
    f0 = r0 // dilation
    if dilation == FOLD_STRIDE:
        for rr in range(dilation):
            for s in range(SLABS):
                slab_ref[slot, s, pl.ds(r0 + rr, n, stride=dilation), :] = (
                    src_ref[0, rr, f0:f0 + n, s * LANES:(s + 1) * LANES].astype(F32))
    else:
        assert dilation == FOLD_STRIDE * FOLD_STRIDE
        for s in range(SLABS):
            for r1 in range(FOLD_STRIDE):
                for r2 in range(FOLD_STRIDE):
                    tmp_ref[slot % 2, s, r1, pl.ds(r2, n, stride=FOLD_STRIDE), :] = (
                        src_ref[0, r1 + FOLD_STRIDE * r2, f0:f0 + n,
                                s * LANES:(s + 1) * LANES].astype(F32))
                slab_ref[slot, s, pl.ds(r0 + r1, SUB_ROWS // FOLD_STRIDE, stride=FOLD_STRIDE), :] = (
                    tmp_ref[slot % 2, s, r1])
    return jnp.concatenate([slab_ref[slot, s, r0:r0 + SUB_ROWS, :] for s in range(SLABS)],
                           axis=1)


def _post_kernel(x_ref, mod_ref, ya_ref, o0_ref, o1_ref, o2_ref, l0_ref, l1_ref, l2_ref,
                 sa_ref, sb_ref, wa_ref, wb_ref, wo_ref, wg_ref, wu_ref, wd_ref, ln_ref,
                 out_ref, h_ref, slab_ref, x1_ref, act_ref, tmp_ref, *, alpha, ff_chunk):
    g1 = mod_ref[0, 2:3, :]
    sh2 = mod_ref[0, 3:4, :]
    s2 = mod_ref[0, 4:5, :]
    g2 = mod_ref[0, 5:6, :]
    d_ff = wg_ref.shape[1]
    held = {}

    def mix(r0):
        slot = 0
        os_, ls_ = [], []
        for (_, dilation), o_ref, l_ref in zip(B_GROUPS, (o0_ref, o1_ref, o2_ref),
                                               (l0_ref, l1_ref, l2_ref)):
            os_.append(_unfold(o_ref, slab_ref, tmp_ref, slot, dilation, r0))
            ls_.append(_unfold(l_ref, slab_ref, tmp_ref, slot + 1, dilation, r0))
            slot += 2 * (dilation > 1)
        lmax = jnp.maximum(jnp.maximum(ls_[0], ls_[1]), ls_[2])
        es = [jnp.exp2(l - lmax) for l in ls_]
        yb = (es[0] * os_[0] + es[1] * os_[1] + es[2] * os_[2]) / (es[0] + es[1] + es[2])
        held[r0, "yb"] = yb.astype(BF16)

    def merge(r0):
        rs = slice(r0, r0 + SUB_ROWS)
        a = jnp.dot(ya_ref[rs, :], wa_ref[...], preferred_element_type=F32)
        bb = jnp.dot(held[r0, "yb"], wb_ref[...], preferred_element_type=F32)
        merged = sa_ref[rs, :].astype(F32) * a + sb_ref[rs, :].astype(F32) * bb
        act_ref[rs, :] = merged.astype(BF16)

    def out_proj(r0):
        rs = slice(r0, r0 + SUB_ROWS)
        held[r0, "t1"] = jnp.dot(act_ref[rs, :], wo_ref[...], preferred_element_type=F32)

    def norm1(r0):
        rs = slice(r0, r0 + SUB_ROWS)
        x1 = _layer_norm(alpha * x_ref[rs, :] + g1 * held[r0, "t1"],
                         ln_ref[0:1, :], ln_ref[1:2, :])
        x1_ref[rs, :] = x1
        act_ref[rs, :] = (x1 * (1.0 + s2) + sh2).astype(BF16)

    def ffn_up(r0, c):
        rs = slice(r0, r0 + SUB_ROWS)
        gate = jnp.dot(act_ref[rs, :], wg_ref[:, c:c + ff_chunk], preferred_element_type=F32)
        up = jnp.dot(act_ref[rs, :], wu_ref[:, c:c + ff_chunk], preferred_element_type=F32)
        h_ref[rs, c:c + ff_chunk] = (gate * _sigmoid(gate) * up).astype(BF16)

    def ffn_down(r0):
        rs = slice(r0, r0 + SUB_ROWS)
        held[r0, "t2"] = jnp.dot(h_ref[rs, :], wd_ref[...], preferred_element_type=F32)

    def norm2(r0):
        rs = slice(r0, r0 + SUB_ROWS)
        out_ref[rs, :] = _layer_norm(alpha * x1_ref[rs, :] + g2 * held[r0, "t2"],
                                     ln_ref[2:3, :], ln_ref[3:4, :])

    first, second = 0, SUB_ROWS
    chunks = list(range(0, d_ff, ff_chunk))
    mix(first); merge(first); mix(second); out_proj(first); merge(second)
    norm1(first); out_proj(second)
    for i, c in enumerate(chunks):
        ffn_up(first, c)
        if i == 2:
            norm1(second)
    ffn_down(first)
    for i, c in enumerate(chunks):
        ffn_up(second, c)
        if i == 2:
            norm2(first)
    ffn_down(second)
    norm2(second)


def _post(x2, mod, ya, obs, lses, sa, sb, weights, layer, seq, alpha):
    t, d = x2.shape
    d_ff = weights[3].shape[2]
    tiles_per_seq = seq // ROW_TILE
    row = lambda w: pl.BlockSpec((ROW_TILE, w), lambda i: (i, 0))
    folded = [_folded_spec(dilation, ROW_TILE, tiles_per_seq) for _, dilation in B_GROUPS]
    n_slabs = 2 * sum(dilation > 1 for _, dilation in B_GROUPS)
    kernel = functools.partial(_post_kernel, alpha=alpha, ff_chunk=256)
    return pl.pallas_call(
        kernel,
        grid=(t // ROW_TILE,),
        in_specs=[row(d), _mod_spec(layer, tiles_per_seq, d),
                  row(A_Q_W)] + folded + folded + [row(d), row(d)] +
                 [_resident(layer, w.shape[1:]) for w in weights],
        out_specs=row(d),
        out_shape=jax.ShapeDtypeStruct((t, d), F32),
        scratch_shapes=[pltpu.VMEM((ROW_TILE, d_ff), BF16),
                        pltpu.VMEM((n_slabs, SLABS, ROW_TILE, LANES), F32),
                        pltpu.VMEM((ROW_TILE, d), F32),
                        pltpu.VMEM((ROW_TILE, d), BF16),
                        pltpu.VMEM((2, SLABS, FOLD_STRIDE, SUB_ROWS // FOLD_STRIDE, LANES), F32)],
        compiler_params=pltpu.CompilerParams(
            dimension_semantics=("arbitrary",), vmem_limit_bytes=VMEM_LIMIT),
        name="merge_ffn",
    )(x2, mod, ya, *obs, *lses, sa, sb, *weights)


def _pair_order(w, axis):
    group = A_Q_HEADS // A_KV_HEADS
    shape = w.shape
    w = w.reshape(shape[:axis] + (A_KV_HEADS, group, HEAD_DIM) + shape[axis + 1:])
    return jnp.swapaxes(w, axis, axis + 1).reshape(shape)


def kernel(x, c, w_ada, b_ada, w_in, sinks, w_a, w_b, w_o, ln1_g, ln1_b,
           w_gate, w_up, w_down, ln2_g, ln2_b):
    bsz, seq, d = x.shape
    depth = w_ada.shape[0]
    alpha = (2 * depth) ** 0.25
    max_dilation = max(dil for _, dil in B_GROUPS)
    assert seq % ATTN_ROWS == 0 and seq % IN_ROWS == 0 and seq % ROW_TILE == 0
    assert seq % (max_dilation * BLOCK) == 0

    slopes = jnp.exp2(-8.0 * jnp.arange(1, N_ATTN_HEADS + 1, dtype=F32) / N_ATTN_HEADS)
    mod = _modulation(c, w_ada, b_ada)
    w_in_b = w_in.astype(BF16)
    wqa_b = _pair_order(w_in[:, :, :A_Q_W], 2).astype(BF16)
    post_weights = (_pair_order(w_a, 1).astype(BF16), w_b.astype(BF16), w_o.astype(BF16),
                    w_gate.astype(BF16), w_up.astype(BF16), w_down.astype(BF16),
                    jnp.stack([ln1_g, ln1_b, ln2_g, ln2_b], axis=1))
    x2 = x.reshape(bsz * seq, d)
    for l in range(depth):
        outs = _inproj(x2, mod, w_in_b, wqa_b, l, bsz, seq)
        qa, kva = outs[0], outs[1]
        sa, sb = outs[-2], outs[-1]
        ya = _attention_a(slopes, sinks[l], qa.reshape(bsz, seq, -1), kva.reshape(bsz, seq, -1))
        obs, lses = [], []
        for g in range(len(B_GROUPS)):
            o, lse = _attention_b(slopes, *outs[2 + 3 * g:5 + 3 * g], g)
            obs.append(o)
            lses.append(lse)
        x2 = _post(x2, mod, ya.reshape(bsz * seq, -1), obs, lses, sa, sb, post_weights, l,
                   seq, alpha)
    return x2.reshape(bsz, seq, d)
```

```python
import functools
import math

import jax
import jax.numpy as jnp
from jax import lax
from jax.experimental import pallas as pl
from jax.experimental.pallas import tpu as pltpu

HEAD_DIM = 64
A_Q_HEADS = 8
A_KV_HEADS = 2
A_WINDOW = 128
B_GROUPS = ((128, 1), (512, 4), (2048, 16))
B_HEADS_PER_GROUP = 4
N_ATTN_HEADS = A_Q_HEADS + B_HEADS_PER_GROUP * len(B_GROUPS)
BLOCK = 128
LN_EPS = 1e-5
NEG_INF = -1e30

LANES = 128
A_Q_W = A_Q_HEADS * HEAD_DIM
A_KV_W = A_KV_HEADS * HEAD_DIM
B_GROUP_W = B_HEADS_PER_GROUP * HEAD_DIM
SLABS = B_GROUP_W // LANES
FOLD_STRIDE = 4
IN_ROWS = 512
ROW_TILE = 512
SUB_ROWS = 256
ATTN_ROWS = 4096
VMEM_LIMIT = 56 * 1024 * 1024
LOG2E = math.log2(math.e)
Q_SCALE = HEAD_DIM ** -0.5 * LOG2E

F32 = jnp.float32
BF16 = jnp.bfloat16


def _sigmoid(x):
    return 0.5 * jnp.tanh(0.5 * x) + 0.5


def _layer_norm(y, g, b):
    mu = jnp.mean(y, axis=-1, keepdims=True)
    yc = y - mu
    var = jnp.mean(yc * yc, axis=-1, keepdims=True)
    return yc * lax.rsqrt(var + LN_EPS) * g + b


def _resident(layer, shape):
    return pl.BlockSpec((None,) + tuple(shape), lambda *_: (layer,) + (0,) * len(shape),
                        pipeline_mode=pl.Buffered(1))


def _mod_spec(layer, tiles_per_seq, d):
    return pl.BlockSpec((None, 1, 6, d), lambda i: (layer, i // tiles_per_seq, 0, 0))


def _folded_spec(dilation, rows, tiles_per_seq):
    return pl.BlockSpec((1, dilation, rows // dilation, B_GROUP_W),
                        lambda i: (i // tiles_per_seq, 0, i % tiles_per_seq, 0))


def _mod_kernel(ct_ref, w_ref, b_ref, o_ref):
    w = w_ref[0]
    for b in range(o_ref.shape[1]):
        c = ct_ref[:, b:b + 1]
        sc = c * _sigmoid(c)
        o_ref[0, b:b + 1, :] = jnp.sum(sc * w, axis=0, keepdims=True) + b_ref[0]


def _modulation(c, w_ada, b_ada):
    depth, d, n = w_ada.shape
    bsz = c.shape[0]
    ct = jnp.pad(c.T, ((0, 0), (0, LANES - bsz)))
    cols = 2 * d
    out = pl.pallas_call(
        _mod_kernel,
        grid=(depth, n // cols),
        in_specs=[pl.BlockSpec((d, LANES), lambda l, j: (0, 0)),
                  pl.BlockSpec((1, d, cols), lambda l, j: (l, 0, j)),
                  pl.BlockSpec((1, 1, cols), lambda l, j: (l, 0, j))],
        out_specs=pl.BlockSpec((1, bsz, cols), lambda l, j: (l, 0, j)),
        out_shape=jax.ShapeDtypeStruct((depth, bsz, n), F32),
        compiler_params=pltpu.CompilerParams(
            dimension_semantics=("arbitrary", "arbitrary"), vmem_limit_bytes=VMEM_LIMIT),
        name="adaln_mod",
    )(ct, w_ada, b_ada.reshape(depth, 1, n))
    return out.reshape(depth, bsz, n // d, d)


def _inproj_kernel(x_ref, mod_ref, w_ref, wqa_ref, qa_ref, kva_ref, *rest):
    n_g = len(B_GROUPS)
    b_refs, (sa_ref, sb_ref, u_ref, fold_ref, tmp_ref) = rest[:3 * n_g], rest[3 * n_g:]
    sh1 = mod_ref[0, 0:1, :]
    s1 = mod_ref[0, 1:2, :]
    u_ref[...] = (x_ref[...] * (1.0 + s1) + sh1).astype(BF16)

    def proj(off, width):
        return jnp.dot(u_ref[...], w_ref[:, off:off + width], preferred_element_type=F32)

    d_model = sa_ref.shape[1]
    kv_off = A_Q_W
    qkv_off = A_Q_W + 2 * A_KV_W
    gate_off = qkv_off + 3 * n_g * B_GROUP_W
    sa_ref[...] = _sigmoid(proj(gate_off, d_model)).astype(BF16)
    sb_ref[...] = _sigmoid(proj(gate_off + d_model, d_model)).astype(BF16)
    slot = slot2 = 0
    for g, (_, dilation) in enumerate(B_GROUPS):
        for t in range(3):
            r = proj(qkv_off + (t * n_g + g) * B_GROUP_W, B_GROUP_W)
            if t == 0:
                r = r * Q_SCALE
            ref = b_refs[3 * g + t]
            if dilation == 1:
                ref[0, 0] = r.astype(BF16)
                continue
            for s in range(SLABS):
                fold_ref[slot, s] = r[:, s * LANES:(s + 1) * LANES]
            if dilation == FOLD_STRIDE:
                for rr in range(dilation):
                    for s in range(SLABS):
                        part = fold_ref[slot, s, pl.ds(rr, IN_ROWS // dilation, stride=dilation), :]
                        ref[0, rr, :, s * LANES:(s + 1) * LANES] = part.astype(BF16)
            else:
                assert dilation == FOLD_STRIDE * FOLD_STRIDE
                for s in range(SLABS):
                    for r1 in range(FOLD_STRIDE):
                        tmp_ref[slot2, s, r1] = fold_ref[
                            slot, s, pl.ds(r1, IN_ROWS // FOLD_STRIDE, stride=FOLD_STRIDE), :]
                for r1 in range(FOLD_STRIDE):
                    for r2 in range(FOLD_STRIDE):
                        for s in range(SLABS):
                            part = tmp_ref[slot2, s, r1,
                                           pl.ds(r2, IN_ROWS // dilation, stride=FOLD_STRIDE), :]
                            ref[0, r1 + FOLD_STRIDE * r2, :, s * LANES:(s + 1) * LANES] = (
                                part.astype(BF16))
                slot2 += 1
            slot += 1
    qa = jnp.dot(u_ref[...], wqa_ref[...], preferred_element_type=F32)
    qa_ref[...] = (qa * Q_SCALE).astype(BF16)
    kva_ref[...] = proj(kv_off, 2 * A_KV_W).astype(BF16)


def _inproj(x2, mod, w_in_b, wqa_b, layer, bsz, seq):
    t, d = x2.shape
    n = w_in_b.shape[2]
    tiles_per_seq = seq // IN_ROWS
    row = lambda w: pl.BlockSpec((IN_ROWS, w), lambda i: (i, 0))
    out_shapes = [jax.ShapeDtypeStruct((t, A_Q_W), BF16),
                  jax.ShapeDtypeStruct((t, 2 * A_KV_W), BF16)]
    out_specs = [row(A_Q_W), row(2 * A_KV_W)]
    n_folded = n_two_pass = 0
    for _, dilation in B_GROUPS:
        for _ in range(3):
            out_shapes.append(jax.ShapeDtypeStruct((bsz, dilation, seq // dilation, B_GROUP_W), BF16))
            out_specs.append(_folded_spec(dilation, IN_ROWS, tiles_per_seq))
            n_folded += dilation > 1
            n_two_pass += dilation > FOLD_STRIDE
    out_shapes += [jax.ShapeDtypeStruct((t, d), BF16)] * 2
    out_specs += [row(d), row(d)]
    return pl.pallas_call(
        _inproj_kernel,
        grid=(t // IN_ROWS,),
        in_specs=[row(d), _mod_spec(layer, tiles_per_seq, d),
                  _resident(layer, (d, n)), _resident(layer, (d, A_Q_W))],
        out_specs=out_specs,
        out_shape=out_shapes,
        scratch_shapes=[pltpu.VMEM((IN_ROWS, d), BF16),
                        pltpu.VMEM((n_folded, SLABS, IN_ROWS, LANES), F32),
                        pltpu.VMEM((n_two_pass, SLABS, FOLD_STRIDE, IN_ROWS // FOLD_STRIDE, LANES),
                                   F32)],
        compiler_params=pltpu.CompilerParams(
            dimension_semantics=("arbitrary",), vmem_limit_bytes=VMEM_LIMIT),
        name="inproj",
    )(x2, mod, w_in_b, wqa_b)


def _fill_bias_tables(tab_ref, slopes_ref, head_ids, max_dist, stride, sinks_ref=None):
    qi = lax.broadcasted_iota(jnp.int32, (BLOCK, 2 * BLOCK), 0)
    sj = lax.broadcasted_iota(jnp.int32, (BLOCK, 2 * BLOCK), 1)
    dist = qi + BLOCK - sj
    valid = (dist >= 0) & (dist <= max_dist)
    valid_first = valid & (sj >= BLOCK)
    dist_f = (dist * stride).astype(F32)
    if sinks_ref is not None:
        assert max_dist < BLOCK
    for h, hid in enumerate(head_ids):
        bias = -(slopes_ref[hid] * dist_f) * LOG2E
        for v, ok in enumerate((valid, valid_first)):
            tab = jnp.where(ok, bias, NEG_INF)
            if sinks_ref is not None:
                tab = jnp.where(sj == 0, sinks_ref[hid] * LOG2E, tab)
            tab_ref[v, h * BLOCK:(h + 1) * BLOCK, :] = tab


def _attend(qs, k2, v2, tab):
    s = lax.dot_general(qs, k2, (((1,), (1,)), ((), ())), preferred_element_type=F32) + tab
    m = jnp.max(s, axis=-1, keepdims=True)
    e = jnp.exp2(s - m).astype(BF16)
    vext = jnp.concatenate([v2, jnp.ones((2 * BLOCK, LANES), BF16)], axis=1)
    acc = jnp.dot(e, vext, preferred_element_type=F32)
    return acc[:, :LANES], acc[:, LANES:], m


def _lane_masks():
    lane = lax.broadcasted_iota(jnp.int32, (1, LANES), 1)
    lo = lane < HEAD_DIM
    return lo, lo.astype(BF16), (~lo).astype(BF16)


def _prev_cur(prev_ref, cur_ref, i, cols, sq=0):
    if i == 0:
        return jnp.concatenate([prev_ref[sq, :, cols], cur_ref[sq, 0:BLOCK, cols]], axis=0)
    return cur_ref[sq, (i - 1) * BLOCK:(i + 1) * BLOCK, cols]


def _attn_a_kernel(slopes_ref, sinks_ref, q_ref, kvc_ref, kvp_ref, o_ref, tab_ref, *, n_blocks):
    b, j = pl.program_id(0), pl.program_id(1)
    group = A_Q_HEADS // A_KV_HEADS
    head_order = tuple(h for p in range(group) for h in (p, p + group))

    @pl.when((b == 0) & (j == 0))
    def _():
        _fill_bias_tables(tab_ref, slopes_ref, head_order, A_WINDOW - 1, 1, sinks_ref)

    lo, mask_lo, mask_hi = _lane_masks()
    first = jnp.where(j == 0, 1, 0)
    not_sink_row = lax.broadcasted_iota(jnp.int32, (2 * BLOCK, 1), 0) > 0
    for i in range(n_blocks):
        rows = slice(i * BLOCK, (i + 1) * BLOCK)
        kv = _prev_cur(kvp_ref, kvc_ref, i, slice(None))
        kv = jnp.where(not_sink_row, kv, jnp.zeros_like(kv))
        for p in range(group):
            cols = slice(p * LANES, (p + 1) * LANES)
            trows = slice(2 * p * BLOCK, (2 * p + 2) * BLOCK)
            q2 = q_ref[0, rows, cols]
            qs = jnp.concatenate([q2 * mask_lo, q2 * mask_hi], axis=0)
            tab = tab_ref[first, trows, :] if i == 0 else tab_ref[0, trows, :]
            acc, den, _ = _attend(qs, kv[:, :LANES], kv[:, LANES:], tab)
            acc, den = (jnp.where(lo, t[:BLOCK], t[BLOCK:]) for t in (acc, den))
            o_ref[0, rows, cols] = (acc / den).astype(BF16)


def _attention_a(slopes, sinks_l, qa, kva):
    bsz, seq, _ = qa.shape
    qb = ATTN_ROWS // BLOCK
    kernel = functools.partial(_attn_a_kernel, n_blocks=qb)
    smem = pl.BlockSpec(memory_space=pltpu.SMEM)
    return pl.pallas_call(
        kernel,
        grid=(bsz, seq // ATTN_ROWS),
        in_specs=[smem, smem,
                  pl.BlockSpec((1, ATTN_ROWS, A_Q_W), lambda b, j: (b, j, 0)),
                  pl.BlockSpec((1, ATTN_ROWS, 2 * A_KV_W), lambda b, j: (b, j, 0)),
                  pl.BlockSpec((1, BLOCK, 2 * A_KV_W),
                               lambda b, j: (b, jnp.maximum(j * qb - 1, 0), 0))],
        out_specs=pl.BlockSpec((1, ATTN_ROWS, A_Q_W), lambda b, j: (b, j, 0)),
        out_shape=jax.ShapeDtypeStruct((bsz, seq, A_Q_W), BF16),
        scratch_shapes=[pltpu.VMEM((2, A_Q_HEADS * BLOCK, 2 * BLOCK), F32)],
        compiler_params=pltpu.CompilerParams(
            dimension_semantics=("arbitrary", "arbitrary"), vmem_limit_bytes=VMEM_LIMIT),
        name="attn_swa",
    )(slopes, sinks_l, qa, kva, kva)


def _attn_b_kernel(slopes_ref, q_ref, kc_ref, kp_ref, vc_ref, vp_ref, o_ref, lse_ref, tab_ref,
                   *, n_seqs, n_blocks, dilation, max_dist, head0):
    b, j = pl.program_id(0), pl.program_id(1)

    @pl.when((b == 0) & (j == 0))
    def _():
        heads = tuple(range(head0, head0 + B_HEADS_PER_GROUP))
        _fill_bias_tables(tab_ref, slopes_ref, heads, max_dist, dilation)

    lo, mask_lo, mask_hi = _lane_masks()
    first = jnp.where(j == 0, 1, 0)
    for sq in range(n_seqs):
        for i in range(n_blocks):
            rows = slice(i * BLOCK, (i + 1) * BLOCK)
            for p in range(B_HEADS_PER_GROUP // 2):
                cols = slice(p * LANES, (p + 1) * LANES)
                trows = slice(2 * p * BLOCK, (2 * p + 2) * BLOCK)
                q2 = q_ref[sq, rows, cols]
                qs = jnp.concatenate([q2 * mask_lo, q2 * mask_hi], axis=0)
                tab = tab_ref[first, trows, :] if i == 0 else tab_ref[0, trows, :]
                acc, den, m = _attend(qs, _prev_cur(kp_ref, kc_ref, i, cols, sq),
                                      _prev_cur(vp_ref, vc_ref, i, cols, sq), tab)
                acc, den, m = (jnp.where(lo, t[:BLOCK], t[BLOCK:]) for t in (acc, den, m))
                o_ref[sq, rows, cols] = (acc / den).astype(BF16)
                lse_ref[sq, rows, cols] = m + jnp.log(den) * LOG2E


def _attention_b(slopes, q, k, v, group):
    window, dilation = B_GROUPS[group]
    bsz, _, n, w = q.shape
    rows = min(ATTN_ROWS, n)
    qb = rows // BLOCK
    n_seqs = ATTN_ROWS // rows
    assert (bsz * dilation) % n_seqs == 0
    q, k, v = (t.reshape(bsz * dilation, n, w) for t in (q, k, v))
    kernel = functools.partial(
        _attn_b_kernel, n_seqs=n_seqs, n_blocks=qb, dilation=dilation,
        max_dist=window // dilation, head0=A_Q_HEADS + group * B_HEADS_PER_GROUP)
    cur = pl.BlockSpec((n_seqs, rows, w), lambda b, j: (b, j, 0))
    prev = pl.BlockSpec((n_seqs, BLOCK, w), lambda b, j: (b, jnp.maximum(j * qb - 1, 0), 0))
    o, lse = pl.pallas_call(
        kernel,
        grid=(bsz * dilation // n_seqs, n // rows),
        in_specs=[pl.BlockSpec(memory_space=pltpu.SMEM), cur, cur, prev, cur, prev],
        out_specs=[cur, cur],
        out_shape=[jax.ShapeDtypeStruct((bsz * dilation, n, w), BF16),
                   jax.ShapeDtypeStruct((bsz * dilation, n, w), F32)],
        scratch_shapes=[pltpu.VMEM((2, B_HEADS_PER_GROUP * BLOCK, 2 * BLOCK), F32)],
        compiler_params=pltpu.CompilerParams(
            dimension_semantics=("arbitrary", "arbitrary"), vmem_limit_bytes=VMEM_LIMIT),
        name=f"attn_dil{dilation}",
    )(slopes, q, k, k, v, v)
    return o.reshape(bsz, dilation, n, w), lse.reshape(bsz, dilation, n, w)


def _unfold(src_ref, slab_ref, tmp_ref, slot, dilation, r0):
    if dilation == 1:
        return src_ref[0, 0, r0:r0 + SUB_ROWS, :].astype(F32)
    n = SUB_ROWS // dilation
    f0 = r0 // dilation
    if dilation == FOLD_STRIDE:
        for rr in range(dilation):
            for s in range(SLABS):
                slab_ref[slot, s, pl.ds(r0 + rr, n, stride=dilation), :] = (
                    src_ref[0, rr, f0:f0 + n, s * LANES:(s + 1) * LANES].astype(F32))
    else:
        assert dilation == FOLD_STRIDE * FOLD_STRIDE
        for s in range(SLABS):
            for r1 in range(FOLD_STRIDE):
                for r2 in range(FOLD_STRIDE):
                    tmp_ref[slot % 2, s, r1, pl.ds(r2, n, stride=FOLD_STRIDE), :] = (
                        src_ref[0, r1 + FOLD_STRIDE * r2, f0:f0 + n,
                                s * LANES:(s + 1) * LANES].astype(F32))
                slab_ref[slot, s, pl.ds(r0 + r1, SUB_ROWS // FOLD_STRIDE, stride=FOLD_STRIDE), :] = (
                    tmp_ref[slot % 2, s, r1])
    return jnp.concatenate([slab_ref[slot, s, r0:r0 + SUB_ROWS, :] for s in range(SLABS)],
                           axis=1)


def _post_kernel(x_ref, mod_ref, ya_ref, o0_ref, o1_ref, o2_ref, l0_ref, l1_ref, l2_ref,
                 sa_ref, sb_ref, wa_ref, wb_ref, wo_ref, wg_ref, wu_ref, wd_ref, ln_ref,
                 out_ref, h_ref, slab_ref, x1_ref, act_ref, tmp_ref, *, alpha, ff_chunk):
    g1 = mod_ref[0, 2:3, :]
    sh2 = mod_ref[0, 3:4, :]
    s2 = mod_ref[0, 4:5, :]
    g2 = mod_ref[0, 5:6, :]
    d_ff = wg_ref.shape[1]
    held = {}

    def mix(r0):
        slot = 0
        os_, ls_ = [], []
        for (_, dilation), o_ref, l_ref in zip(B_GROUPS, (o0_ref, o1_ref, o2_ref),
                                               (l0_ref, l1_ref, l2_ref)):
            os_.append(_unfold(o_ref, slab_ref, tmp_ref, slot, dilation, r0))
            ls_.append(_unfold(l_ref, slab_ref, tmp_ref, slot + 1, dilation, r0))
            slot += 2 * (dilation > 1)
        lmax = jnp.maximum(jnp.maximum(ls_[0], ls_[1]), ls_[2])
        es = [jnp.exp2(l - lmax) for l in ls_]
        yb = (es[0] * os_[0] + es[1] * os_[1] + es[2] * os_[2]) / (es[0] + es[1] + es[2])
        held[r0, "yb"] = yb.astype(BF16)

    def merge(r0):
        rs = slice(r0, r0 + SUB_ROWS)
        a = jnp.dot(ya_ref[rs, :], wa_ref[...], preferred_element_type=F32)
        bb = jnp.dot(held[r0, "yb"], wb_ref[...], preferred_element_type=F32)
        merged = sa_ref[rs, :].astype(F32) * a + sb_ref[rs, :].astype(F32) * bb
        act_ref[rs, :] = merged.astype(BF16)

    def out_proj(r0):
        rs = slice(r0, r0 + SUB_ROWS)
        held[r0, "t1"] = jnp.dot(act_ref[rs, :], wo_ref[...], preferred_element_type=F32)

    def norm1(r0):
        rs = slice(r0, r0 + SUB_ROWS)
        x1 = _layer_norm(alpha * x_ref[rs, :] + g1 * held[r0, "t1"],
                         ln_ref[0:1, :], ln_ref[1:2, :])
        x1_ref[rs, :] = x1
        act_ref[rs, :] = (x1 * (1.0 + s2) + sh2).astype(BF16)

    def ffn_up(r0, c):
        rs = slice(r0, r0 + SUB_ROWS)
        gate = jnp.dot(act_ref[rs, :], wg_ref[:, c:c + ff_chunk], preferred_element_type=F32)
        up = jnp.dot(act_ref[rs, :], wu_ref[:, c:c + ff_chunk], preferred_element_type=F32)
        h_ref[rs, c:c + ff_chunk] = (gate * _sigmoid(gate) * up).astype(BF16)

    def ffn_down(r0):
        rs = slice(r0, r0 + SUB_ROWS)
        held[r0, "t2"] = jnp.dot(h_ref[rs, :], wd_ref[...], preferred_element_type=F32)

    def norm2(r0):
        rs = slice(r0, r0 + SUB_ROWS)
        out_ref[rs, :] = _layer_norm(alpha * x1_ref[rs, :] + g2 * held[r0, "t2"],
                                     ln_ref[2:3, :], ln_ref[3:4, :])

    chunks = list(range(0, d_ff, ff_chunk))
    third = -(-len(chunks) // 3)

    def ffn_up_part(k):
        return lambda r0: [ffn_up(r0, c) for c in chunks[k * third:(k + 1) * third]]

    phases = [mix, merge, out_proj, norm1, ffn_up_part(0), ffn_up_part(1), ffn_up_part(2),
              ffn_down, norm2]
    n_sub = ROW_TILE // SUB_ROWS
    for t in range(len(phases) + n_sub - 1):
        for k in range(n_sub):
            if 0 <= t - k < len(phases):
                phases[t - k](k * SUB_ROWS)


def _post(x2, mod, ya, obs, lses, sa, sb, weights, layer, seq, alpha):
    t, d = x2.shape
    d_ff = weights[3].shape[2]
    tiles_per_seq = seq // ROW_TILE
    row = lambda w: pl.BlockSpec((ROW_TILE, w), lambda i: (i, 0))
    folded = [_folded_spec(dilation, ROW_TILE, tiles_per_seq) for _, dilation in B_GROUPS]
    n_slabs = 2 * sum(dilation > 1 for _, dilation in B_GROUPS)
    kernel = functools.partial(_post_kernel, alpha=alpha, ff_chunk=256)
    return pl.pallas_call(
        kernel,
        grid=(t // ROW_TILE,),
        in_specs=[row(d), _mod_spec(layer, tiles_per_seq, d),
                  row(A_Q_W)] + folded + folded + [row(d), row(d)] +
                 [_resident(layer, w.shape[1:]) for w in weights],
        out_specs=row(d),
        out_shape=jax.ShapeDtypeStruct((t, d), F32),
        scratch_shapes=[pltpu.VMEM((ROW_TILE, d_ff), BF16),
                        pltpu.VMEM((n_slabs, SLABS, ROW_TILE, LANES), F32),
                        pltpu.VMEM((ROW_TILE, d), F32),
                        pltpu.VMEM((ROW_TILE, d), BF16),
                        pltpu.VMEM((2, SLABS, FOLD_STRIDE, SUB_ROWS // FOLD_STRIDE, LANES), F32)],
        compiler_params=pltpu.CompilerParams(
            dimension_semantics=("arbitrary",), vmem_limit_bytes=VMEM_LIMIT),
        name="merge_ffn",
    )(x2, mod, ya, *obs, *lses, sa, sb, *weights)


def _pair_order(w, axis):
    group = A_Q_HEADS // A_KV_HEADS
    shape = w.shape
    w = w.reshape(shape[:axis] + (A_KV_HEADS, group, HEAD_DIM) + shape[axis + 1:])
    return jnp.swapaxes(w, axis, axis + 1).reshape(shape)


def kernel(x, c, w_ada, b_ada, w_in, sinks, w_a, w_b, w_o, ln1_g, ln1_b,
           w_gate, w_up, w_down, ln2_g, ln2_b):
    bsz, seq, d = x.shape
    depth = w_ada.shape[0]
    alpha = (2 * depth) ** 0.25
    max_dilation = max(dil for _, dil in B_GROUPS)
    assert seq % ATTN_ROWS == 0 and seq % IN_ROWS == 0 and seq % ROW_TILE == 0
    assert seq % (max_dilation * BLOCK) == 0

    slopes = jnp.exp2(-8.0 * jnp.arange(1, N_ATTN_HEADS + 1, dtype=F32) / N_ATTN_HEADS)
    mod = _modulation(c, w_ada, b_ada)
    w_in_b = w_in.astype(BF16)
    wqa_b = _pair_order(w_in[:, :, :A_Q_W], 2).astype(BF16)
    post_weights = (_pair_order(w_a, 1).astype(BF16), w_b.astype(BF16), w_o.astype(BF16),
                    w_gate.astype(BF16), w_up.astype(BF16), w_down.astype(BF16),
                    jnp.stack([ln1_g, ln1_b, ln2_g, ln2_b], axis=1))
    x2 = x.reshape(bsz * seq, d)
    for l in range(depth):
        outs = _inproj(x2, mod, w_in_b, wqa_b, l, bsz, seq)
        qa, kva = outs[0], outs[1]
        sa, sb = outs[-2], outs[-1]
        ya = _attention_a(slopes, sinks[l], qa.reshape(bsz, seq, -1), kva.reshape(bsz, seq, -1))
        obs, lses = [], []
        for g in range(len(B_GROUPS)):
            o, lse = _attention_b(slopes, *outs[2 + 3 * g:5 + 3 * g], g)
            obs.append(o)
            lses.append(lse)
        x2 = _post(x2, mod, ya.reshape(bsz * seq, -1), obs, lses, sa, sb, post_weights, l,
                   seq, alpha)
    return x2.reshape(bsz, seq, d)
```

```python
import functools
import math

import jax
import jax.numpy as jnp
from jax import lax
from jax.experimental import pallas as pl
from jax.experimental.pallas import tpu as pltpu

HEAD_DIM = 64
A_Q_HEADS = 8
A_KV_HEADS = 2
A_WINDOW = 128
B_GROUPS = ((128, 1), (512, 4), (2048, 16))
B_HEADS_PER_GROUP = 4
N_ATTN_HEADS = A_Q_HEADS + B_HEADS_PER_GROUP * len(B_GROUPS)
BLOCK = 128
LN_EPS = 1e-5
NEG_INF = -1e30

LANES = 128
A_Q_W = A_Q_HEADS * HEAD_DIM
A_KV_W = A_KV_HEADS * HEAD_DIM
A_QKV_W = A_Q_W + 2 * A_KV_W
B_GROUP_W = B_HEADS_PER_GROUP * HEAD_DIM
SLABS = B_GROUP_W // LANES
FOLD_STRIDE = 4
IN_ROWS = 1024
ROW_TILE = 512
SUB_ROWS = 256
ATTN_ROWS = 4096
VMEM_LIMIT = 56 * 1024 * 1024
LOG2E = math.log2(math.e)
Q_SCALE = HEAD_DIM ** -0.5 * LOG2E

F32 = jnp.float32
BF16 = jnp.bfloat16


def _sigmoid(x):
    return 0.5 * jnp.tanh(0.5 * x) + 0.5


def _layer_norm(y, g, b):
    mu = jnp.mean(y, axis=-1, keepdims=True)
    yc = y - mu
    var = jnp.mean(yc * yc, axis=-1, keepdims=True)
    return yc * lax.rsqrt(var + LN_EPS) * g + b


def _resident(layer, shape):
    return pl.BlockSpec((None,) + tuple(shape), lambda *_: (layer,) + (0,) * len(shape),
                        pipeline_mode=pl.Buffered(1))


def _mod_spec(layer, tiles_per_seq, d):
    return pl.BlockSpec((None, 1, 6, d), lambda i: (layer, i // tiles_per_seq, 0, 0))


def _folded_spec(dilation, rows, tiles_per_seq, width=B_GROUP_W):
    return pl.BlockSpec((1, dilation, rows // dilation, width),
                        lambda i: (i // tiles_per_seq, 0, i % tiles_per_seq, 0))


def _mod_kernel(ct_ref, w_ref, b_ref, o_ref):
    w = w_ref[0]
    for b in range(o_ref.shape[1]):
        c = ct_ref[:, b:b + 1]
        sc = c * _sigmoid(c)
        o_ref[0, b:b + 1, :] = jnp.sum(sc * w, axis=0, keepdims=True) + b_ref[0]


def _modulation(c, w_ada, b_ada):
    depth, d, n = w_ada.shape
    bsz = c.shape[0]
    ct = jnp.pad(c.T, ((0, 0), (0, LANES - bsz)))
    cols = 2 * d
    out = pl.pallas_call(
        _mod_kernel,
        grid=(depth, n // cols),
        in_specs=[pl.BlockSpec((d, LANES), lambda l, j: (0, 0)),
                  pl.BlockSpec((1, d, cols), lambda l, j: (l, 0, j)),
                  pl.BlockSpec((1, 1, cols), lambda l, j: (l, 0, j))],
        out_specs=pl.BlockSpec((1, bsz, cols), lambda l, j: (l, 0, j)),
        out_shape=jax.ShapeDtypeStruct((depth, bsz, n), F32),
        compiler_params=pltpu.CompilerParams(
            dimension_semantics=("arbitrary", "arbitrary"), vmem_limit_bytes=VMEM_LIMIT),
        name="adaln_mod",
    )(ct, w_ada, b_ada.reshape(depth, 1, n))
    return out.reshape(depth, bsz, n // d, d)


def _inproj_kernel(x_ref, mod_ref, w_ref, wqa_ref, qkva_ref, *rest):
    n_g = len(B_GROUPS)
    b_refs, (gates_ref, u_ref, fold_ref, tmp_ref) = rest[:n_g], rest[n_g:]
    sh1 = mod_ref[0, 0:1, :]
    s1 = mod_ref[0, 1:2, :]
    u_ref[...] = (x_ref[...] * (1.0 + s1) + sh1).astype(BF16)

    def proj(off, width):
        return jnp.dot(u_ref[...], w_ref[:, off:off + width], preferred_element_type=F32)

    d_model = gates_ref.shape[1] // 2
    kv_off = A_Q_W
    qkv_off = A_Q_W + 2 * A_KV_W
    gate_off = qkv_off + 3 * n_g * B_GROUP_W
    gates_ref[:, :d_model] = _sigmoid(proj(gate_off, d_model)).astype(BF16)
    gates_ref[:, d_model:] = _sigmoid(proj(gate_off + d_model, d_model)).astype(BF16)
    slot = slot2 = 0
    for g, (_, dilation) in enumerate(B_GROUPS):
        ref = b_refs[g]
        for t in range(3):
            r = proj(qkv_off + (t * n_g + g) * B_GROUP_W, B_GROUP_W)
            if t == 0:
                r = r * Q_SCALE
            c0 = t * B_GROUP_W
            if dilation == 1:
                ref[0, 0, :, c0:c0 + B_GROUP_W] = r.astype(BF16)
                continue
            for s in range(SLABS):
                fold_ref[slot, s] = r[:, s * LANES:(s + 1) * LANES]
            if dilation == FOLD_STRIDE:
                for rr in range(dilation):
                    for s in range(SLABS):
                        part = fold_ref[slot, s, pl.ds(rr, IN_ROWS // dilation, stride=dilation), :]
                        ref[0, rr, :, c0 + s * LANES:c0 + (s + 1) * LANES] = part.astype(BF16)
            else:
                assert dilation == FOLD_STRIDE * FOLD_STRIDE
                for s in range(SLABS):
                    for r1 in range(FOLD_STRIDE):
                        tmp_ref[slot2, s, r1] = fold_ref[
                            slot, s, pl.ds(r1, IN_ROWS // FOLD_STRIDE, stride=FOLD_STRIDE), :]
                for r1 in range(FOLD_STRIDE):
                    for r2 in range(FOLD_STRIDE):
                        for s in range(SLABS):
                            part = tmp_ref[slot2, s, r1,
                                           pl.ds(r2, IN_ROWS // dilation, stride=FOLD_STRIDE), :]
                            ref[0, r1 + FOLD_STRIDE * r2, :,
                                c0 + s * LANES:c0 + (s + 1) * LANES] = part.astype(BF16)
                slot2 += 1
            slot += 1
    qa = jnp.dot(u_ref[...], wqa_ref[...], preferred_element_type=F32)
    qkva_ref[:, :A_Q_W] = (qa * Q_SCALE).astype(BF16)
    qkva_ref[:, A_Q_W:] = proj(kv_off, 2 * A_KV_W).astype(BF16)


def _inproj(x2, mod, w_in_b, wqa_b, layer, bsz, seq):
    t, d = x2.shape
    n = w_in_b.shape[2]
    tiles_per_seq = seq // IN_ROWS
    row = lambda w: pl.BlockSpec((IN_ROWS, w), lambda i: (i, 0))
    out_shapes = [jax.ShapeDtypeStruct((t, A_QKV_W), BF16)]
    out_specs = [row(A_QKV_W)]
    n_folded = n_two_pass = 0
    for _, dilation in B_GROUPS:
        out_shapes.append(jax.ShapeDtypeStruct((bsz, dilation, seq // dilation, 3 * B_GROUP_W), BF16))
        out_specs.append(_folded_spec(dilation, IN_ROWS, tiles_per_seq, 3 * B_GROUP_W))
        n_folded += 3 * (dilation > 1)
        n_two_pass += 3 * (dilation > FOLD_STRIDE)
    out_shapes.append(jax.ShapeDtypeStruct((t, 2 * d), BF16))
    out_specs.append(row(2 * d))
    return pl.pallas_call(
        _inproj_kernel,
        grid=(t // IN_ROWS,),
        in_specs=[row(d), _mod_spec(layer, tiles_per_seq, d),
                  _resident(layer, (d, n)), _resident(layer, (d, A_Q_W))],
        out_specs=out_specs,
        out_shape=out_shapes,
        scratch_shapes=[pltpu.VMEM((IN_ROWS, d), BF16),
                        pltpu.VMEM((n_folded, SLABS, IN_ROWS, LANES), F32),
                        pltpu.VMEM((n_two_pass, SLABS, FOLD_STRIDE, IN_ROWS // FOLD_STRIDE, LANES),
                                   F32)],
        compiler_params=pltpu.CompilerParams(
            dimension_semantics=("arbitrary",), vmem_limit_bytes=VMEM_LIMIT),
        name="inproj",
    )(x2, mod, w_in_b, wqa_b)


def _fill_bias_tables(tab_ref, slopes_ref, head_ids, max_dist, stride, sinks_ref=None):
    qi = lax.broadcasted_iota(jnp.int32, (BLOCK, 2 * BLOCK), 0)
    sj = lax.broadcasted_iota(jnp.int32, (BLOCK, 2 * BLOCK), 1)
    dist = qi + BLOCK - sj
    valid = (dist >= 0) & (dist <= max_dist)
    valid_first = valid & (sj >= BLOCK)
    dist_f = (dist * stride).astype(F32)
    if sinks_ref is not None:
        assert max_dist < BLOCK
    for h, hid in enumerate(head_ids):
        bias = -(slopes_ref[hid] * dist_f) * LOG2E
        for v, ok in enumerate((valid, valid_first)):
            tab = jnp.where(ok, bias, NEG_INF)
            if sinks_ref is not None:
                tab = jnp.where(sj == 0, sinks_ref[hid] * LOG2E, tab)
            tab_ref[v, h * BLOCK:(h + 1) * BLOCK, :] = tab


def _attend(qs, k2, v2, tab):
    s = lax.dot_general(qs, k2, (((1,), (1,)), ((), ())), preferred_element_type=F32) + tab
    m = jnp.max(s, axis=-1, keepdims=True)
    e = jnp.exp2(s - m).astype(BF16)
    vext = jnp.concatenate([v2, jnp.ones((2 * BLOCK, LANES), BF16)], axis=1)
    acc = jnp.dot(e, vext, preferred_element_type=F32)
    return acc[:, :LANES], acc[:, LANES:], m


def _lane_masks():
    lane = lax.broadcasted_iota(jnp.int32, (1, LANES), 1)
    lo = lane < HEAD_DIM
    return lo, lo.astype(BF16), (~lo).astype(BF16)


def _prev_cur(prev_ref, cur_ref, i, cols, sq=0):
    if i == 0:
        return jnp.concatenate([prev_ref[sq, :, cols], cur_ref[sq, 0:BLOCK, cols]], axis=0)
    return cur_ref[sq, (i - 1) * BLOCK:(i + 1) * BLOCK, cols]


def _attn_a_kernel(slopes_ref, sinks_ref, q_ref, kvc_ref, kvp_ref, o_ref, tab_ref, *, n_blocks):
    b, j = pl.program_id(0), pl.program_id(1)
    group = A_Q_HEADS // A_KV_HEADS
    head_order = tuple(h for p in range(group) for h in (p, p + group))

    @pl.when((b == 0) & (j == 0))
    def _():
        _fill_bias_tables(tab_ref, slopes_ref, head_order, A_WINDOW - 1, 1, sinks_ref)

    lo, mask_lo, mask_hi = _lane_masks()
    first = jnp.where(j == 0, 1, 0)
    not_sink_row = lax.broadcasted_iota(jnp.int32, (2 * BLOCK, 1), 0) > 0
    for i in range(n_blocks):
        rows = slice(i * BLOCK, (i + 1) * BLOCK)
        kv = _prev_cur(kvp_ref, kvc_ref, i, slice(None))
        kv = jnp.where(not_sink_row, kv, jnp.zeros_like(kv))
        for p in range(group):
            cols = slice(p * LANES, (p + 1) * LANES)
            trows = slice(2 * p * BLOCK, (2 * p + 2) * BLOCK)
            q2 = q_ref[0, rows, cols]
            qs = jnp.concatenate([q2 * mask_lo, q2 * mask_hi], axis=0)
            tab = tab_ref[first, trows, :] if i == 0 else tab_ref[0, trows, :]
            acc, den, _ = _attend(qs, kv[:, :LANES], kv[:, LANES:], tab)
            acc, den = (jnp.where(lo, t[:BLOCK], t[BLOCK:]) for t in (acc, den))
            o_ref[0, rows, cols] = (acc / den).astype(BF16)


def _attention_a(slopes, sinks_l, qkva):
    bsz, seq, _ = qkva.shape
    qb = ATTN_ROWS // BLOCK
    kernel = functools.partial(_attn_a_kernel, n_blocks=qb)
    smem = pl.BlockSpec(memory_space=pltpu.SMEM)
    kv_block = A_Q_W // (2 * A_KV_W)
    return pl.pallas_call(
        kernel,
        grid=(bsz, seq // ATTN_ROWS),
        in_specs=[smem, smem,
                  pl.BlockSpec((1, ATTN_ROWS, A_Q_W), lambda b, j: (b, j, 0)),
                  pl.BlockSpec((1, ATTN_ROWS, 2 * A_KV_W), lambda b, j: (b, j, kv_block)),
                  pl.BlockSpec((1, BLOCK, 2 * A_KV_W),
                               lambda b, j: (b, jnp.maximum(j * qb - 1, 0), kv_block))],
        out_specs=pl.BlockSpec((1, ATTN_ROWS, A_Q_W), lambda b, j: (b, j, 0)),
        out_shape=jax.ShapeDtypeStruct((bsz, seq, A_Q_W), BF16),
        scratch_shapes=[pltpu.VMEM((2, A_Q_HEADS * BLOCK, 2 * BLOCK), F32)],
        compiler_params=pltpu.CompilerParams(
            dimension_semantics=("arbitrary", "arbitrary"), vmem_limit_bytes=VMEM_LIMIT),
        name="attn_swa",
    )(slopes, sinks_l, qkva, qkva, qkva)


def _attn_b_kernel(slopes_ref, q_ref, kc_ref, kp_ref, vc_ref, vp_ref, o_ref, lse_ref, tab_ref,
                   *, n_seqs, n_blocks, dilation, max_dist, head0):
    b, j = pl.program_id(0), pl.program_id(1)

    @pl.when((b == 0) & (j == 0))
    def _():
        heads = tuple(range(head0, head0 + B_HEADS_PER_GROUP))
        _fill_bias_tables(tab_ref, slopes_ref, heads, max_dist, dilation)

    lo, mask_lo, mask_hi = _lane_masks()
    first = jnp.where(j == 0, 1, 0)
    for sq in range(n_seqs):
        for i in range(n_blocks):
            rows = slice(i * BLOCK, (i + 1) * BLOCK)
            for p in range(B_HEADS_PER_GROUP // 2):
                cols = slice(p * LANES, (p + 1) * LANES)
                trows = slice(2 * p * BLOCK, (2 * p + 2) * BLOCK)
                q2 = q_ref[sq, rows, cols]
                qs = jnp.concatenate([q2 * mask_lo, q2 * mask_hi], axis=0)
                tab = tab_ref[first, trows, :] if i == 0 else tab_ref[0, trows, :]
                acc, den, m = _attend(qs, _prev_cur(kp_ref, kc_ref, i, cols, sq),
                                      _prev_cur(vp_ref, vc_ref, i, cols, sq), tab)
                acc, den, m = (jnp.where(lo, t[:BLOCK], t[BLOCK:]) for t in (acc, den, m))
                o_ref[sq, rows, cols] = (acc / den).astype(BF16)
                lse_ref[sq, rows, cols] = m + jnp.log(den) * LOG2E


def _attention_b(slopes, qkv, group):
    window, dilation = B_GROUPS[group]
    bsz, _, n, _ = qkv.shape
    w = B_GROUP_W
    rows = min(ATTN_ROWS, n)
    qb = rows // BLOCK
    n_seqs = ATTN_ROWS // rows
    assert (bsz * dilation) % n_seqs == 0
    qkv = qkv.reshape(bsz * dilation, n, 3 * w)
    kernel = functools.partial(
        _attn_b_kernel, n_seqs=n_seqs, n_blocks=qb, dilation=dilation,
        max_dist=window // dilation, head0=A_Q_HEADS + group * B_HEADS_PER_GROUP)
    cur = lambda col: pl.BlockSpec((n_seqs, rows, w), lambda b, j: (b, j, col))
    prev = lambda col: pl.BlockSpec((n_seqs, BLOCK, w),
                                    lambda b, j: (b, jnp.maximum(j * qb - 1, 0), col))
    o, lse = pl.pallas_call(
        kernel,
        grid=(bsz * dilation // n_seqs, n // rows),
        in_specs=[pl.BlockSpec(memory_space=pltpu.SMEM), cur(0), cur(1), prev(1), cur(2), prev(2)],
        out_specs=[cur(0), cur(0)],
        out_shape=[jax.ShapeDtypeStruct((bsz * dilation, n, w), BF16),
                   jax.ShapeDtypeStruct((bsz * dilation, n, w), F32)],
        scratch_shapes=[pltpu.VMEM((2, B_HEADS_PER_GROUP * BLOCK, 2 * BLOCK), F32)],
        compiler_params=pltpu.CompilerParams(
            dimension_semantics=("arbitrary", "arbitrary"), vmem_limit_bytes=VMEM_LIMIT),
        name=f"attn_dil{dilation}",
    )(slopes, qkv, qkv, qkv, qkv, qkv)
    return o.reshape(bsz, dilation, n, w), lse.reshape(bsz, dilation, n, w)


def _unfold(src_ref, slab_ref, tmp_ref, slot, dilation, r0):
    if dilation == 1:
        return src_ref[0, 0, r0:r0 + SUB_ROWS, :].astype(F32)
    n = SUB_ROWS // dilation
    f0 = r0 // dilation
    if dilation == FOLD_STRIDE:
        for rr in range(dilation):
            for s in range(SLABS):
                slab_ref[slot, s, pl.ds(r0 + rr, n, stride=dilation), :] = (
                    src_ref[0, rr, f0:f0 + n, s * LANES:(s + 1) * LANES].astype(F32))
    else:
        assert dilation == FOLD_STRIDE * FOLD_STRIDE
        for s in range(SLABS):
            for r1 in range(FOLD_STRIDE):
                for r2 in range(FOLD_STRIDE):
                    tmp_ref[slot % 2, s, r1, pl.ds(r2, n, stride=FOLD_STRIDE), :] = (
                        src_ref[0, r1 + FOLD_STRIDE * r2, f0:f0 + n,
                                s * LANES:(s + 1) * LANES].astype(F32))
                slab_ref[slot, s, pl.ds(r0 + r1, SUB_ROWS // FOLD_STRIDE, stride=FOLD_STRIDE), :] = (
                    tmp_ref[slot % 2, s, r1])
    return jnp.concatenate([slab_ref[slot, s, r0:r0 + SUB_ROWS, :] for s in range(SLABS)],
                           axis=1)


def _post_kernel(x_ref, mod_ref, ya_ref, o0_ref, o1_ref, o2_ref, l0_ref, l1_ref, l2_ref,
                 gates_ref, wa_ref, wb_ref, wo_ref, wg_ref, wu_ref, wd_ref, ln_ref,
                 out_ref, h_ref, slab_ref, x1_ref, act_ref, tmp_ref, *, alpha, ff_chunk):
    d_model = x_ref.shape[1]
    g1 = mod_ref[0, 2:3, :]
    sh2 = mod_ref[0, 3:4, :]
    s2 = mod_ref[0, 4:5, :]
    g2 = mod_ref[0, 5:6, :]
    d_ff = wg_ref.shape[1]
    held = {}

    def mix(r0):
        slot = 0
        os_, ls_ = [], []
        for (_, dilation), o_ref, l_ref in zip(B_GROUPS, (o0_ref, o1_ref, o2_ref),
                                               (l0_ref, l1_ref, l2_ref)):
            os_.append(_unfold(o_ref, slab_ref, tmp_ref, slot, dilation, r0))
            ls_.append(_unfold(l_ref, slab_ref, tmp_ref, slot + 1, dilation, r0))
            slot += 2 * (dilation > 1)
        lmax = jnp.maximum(jnp.maximum(ls_[0], ls_[1]), ls_[2])
        es = [jnp.exp2(l - lmax) for l in ls_]
        yb = (es[0] * os_[0] + es[1] * os_[1] + es[2] * os_[2]) / (es[0] + es[1] + es[2])
        held[r0, "yb"] = yb.astype(BF16)

    def merge(r0):
        rs = slice(r0, r0 + SUB_ROWS)
        a = jnp.dot(ya_ref[rs, :], wa_ref[...], preferred_element_type=F32)
        bb = jnp.dot(held[r0, "yb"], wb_ref[...], preferred_element_type=F32)
        merged = (gates_ref[rs, :d_model].astype(F32) * a
                  + gates_ref[rs, d_model:].astype(F32) * bb)
        act_ref[rs, :] = merged.astype(BF16)

    def out_proj(r0):
        rs = slice(r0, r0 + SUB_ROWS)
        held[r0, "t1"] = jnp.dot(act_ref[rs, :], wo_ref[...], preferred_element_type=F32)

    def norm1(r0):
        rs = slice(r0, r0 + SUB_ROWS)
        x1 = _layer_norm(alpha * x_ref[rs, :] + g1 * held[r0, "t1"],
                         ln_ref[0:1, :], ln_ref[1:2, :])
        x1_ref[rs, :] = x1
        act_ref[rs, :] = (x1 * (1.0 + s2) + sh2).astype(BF16)

    def ffn_up(r0, c):
        rs = slice(r0, r0 + SUB_ROWS)
        gate = jnp.dot(act_ref[rs, :], wg_ref[:, c:c + ff_chunk], preferred_element_type=F32)
        up = jnp.dot(act_ref[rs, :], wu_ref[:, c:c + ff_chunk], preferred_element_type=F32)
        h_ref[rs, c:c + ff_chunk] = (gate * _sigmoid(gate) * up).astype(BF16)

    def ffn_down(r0):
        rs = slice(r0, r0 + SUB_ROWS)
        held[r0, "t2"] = jnp.dot(h_ref[rs, :], wd_ref[...], preferred_element_type=F32)

    def norm2(r0):
        rs = slice(r0, r0 + SUB_ROWS)
        out_ref[rs, :] = _layer_norm(alpha * x1_ref[rs, :] + g2 * held[r0, "t2"],
                                     ln_ref[2:3, :], ln_ref[3:4, :])

    chunks = list(range(0, d_ff, ff_chunk))
    third = -(-len(chunks) // 3)

    def ffn_up_part(k):
        return lambda r0: [ffn_up(r0, c) for c in chunks[k * third:(k + 1) * third]]

    phases = [mix, merge, out_proj, norm1, ffn_up_part(0), ffn_up_part(1), ffn_up_part(2),
              ffn_down, norm2]
    n_sub = ROW_TILE // SUB_ROWS
    for t in range(len(phases) + n_sub - 1):
        for k in range(n_sub):
            if 0 <= t - k < len(phases):
                phases[t - k](k * SUB_ROWS)


def _post(x2, mod, ya, obs, lses, gates, weights, layer, seq, alpha):
    t, d = x2.shape
    d_ff = weights[3].shape[2]
    tiles_per_seq = seq // ROW_TILE
    row = lambda w: pl.BlockSpec((ROW_TILE, w), lambda i: (i, 0))
    folded = [_folded_spec(dilation, ROW_TILE, tiles_per_seq) for _, dilation in B_GROUPS]
    n_slabs = 2 * sum(dilation > 1 for _, dilation in B_GROUPS)
    kernel = functools.partial(_post_kernel, alpha=alpha, ff_chunk=256)
    return pl.pallas_call(
        kernel,
        grid=(t // ROW_TILE,),
        in_specs=[row(d), _mod_spec(layer, tiles_per_seq, d),
                  row(A_Q_W)] + folded + folded + [row(2 * d)] +
                 [_resident(layer, w.shape[1:]) for w in weights],
        out_specs=row(d),
        out_shape=jax.ShapeDtypeStruct((t, d), F32),
        scratch_shapes=[pltpu.VMEM((ROW_TILE, d_ff), BF16),
                        pltpu.VMEM((n_slabs, SLABS, ROW_TILE, LANES), F32),
                        pltpu.VMEM((ROW_TILE, d), F32),
                        pltpu.VMEM((ROW_TILE, d), BF16),
                        pltpu.VMEM((2, SLABS, FOLD_STRIDE, SUB_ROWS // FOLD_STRIDE, LANES), F32)],
        compiler_params=pltpu.CompilerParams(
            dimension_semantics=("arbitrary",), vmem_limit_bytes=VMEM_LIMIT),
        name="merge_ffn",
    )(x2, mod, ya, *obs, *lses, gates, *weights)


def _pair_order(w, axis):
    group = A_Q_HEADS // A_KV_HEADS
    shape = w.shape
    w = w.reshape(shape[:axis] + (A_KV_HEADS, group, HEAD_DIM) + shape[axis + 1:])
    return jnp.swapaxes(w, axis, axis + 1).reshape(shape)


def kernel(x, c, w_ada, b_ada, w_in, sinks, w_a, w_b, w_o, ln1_g, ln1_b,
           w_gate, w_up, w_down, ln2_g, ln2_b):
    bsz, seq, d = x.shape
    depth = w_ada.shape[0]
    alpha = (2 * depth) ** 0.25
    max_dilation = max(dil for _, dil in B_GROUPS)
    assert seq % ATTN_ROWS == 0 and seq % IN_ROWS == 0 and seq % ROW_TILE == 0
    assert seq % (max_dilation * BLOCK) == 0

    slopes = jnp.exp2(-8.0 * jnp.arange(1, N_ATTN_HEADS + 1, dtype=F32) / N_ATTN_HEADS)
    mod = _modulation(c, w_ada, b_ada)
    w_in_b = w_in.astype(BF16)
    wqa_b = _pair_order(w_in[:, :, :A_Q_W], 2).astype(BF16)
    post_weights = (_pair_order(w_a, 1).astype(BF16), w_b.astype(BF16), w_o.astype(BF16),
                    w_gate.astype(BF16), w_up.astype(BF16), w_down.astype(BF16),
                    jnp.stack([ln1_g, ln1_b, ln2_g, ln2_b], axis=1))
    x2 = x.reshape(bsz * seq, d)
    for l in range(depth):
        qkva, *qkv_groups, gates = _inproj(x2, mod, w_in_b, wqa_b, l, bsz, seq)
        ya = _attention_a(slopes, sinks[l], qkva.reshape(bsz, seq, -1))
        obs, lses = [], []
        for g, qkv in enumerate(qkv_groups):
            o, lse = _attention_b(slopes, qkv, g)
            obs.append(o)
            lses.append(lse)
        x2 = _post(x2, mod, ya.reshape(bsz * seq, -1), obs, lses, gates, post_weights, l,
                   seq, alpha)
    return x2.reshape(bsz, seq, d)
```

```python
import functools
import math

import jax
import jax.numpy as jnp
from jax import lax
from jax.experimental import pallas as pl
from jax.experimental.pallas import tpu as pltpu

HEAD_DIM = 64
A_Q_HEADS = 8
A_KV_HEADS = 2
A_WINDOW = 128
B_GROUPS = ((128, 1), (512, 4), (2048, 16))
B_HEADS_PER_GROUP = 4
N_ATTN_HEADS = A_Q_HEADS + B_HEADS_PER_GROUP * len(B_GROUPS)
BLOCK = 128
LN_EPS = 1e-5
NEG_INF = -1e30

LANES = 128
A_Q_W = A_Q_HEADS * HEAD_DIM
A_KV_W = A_KV_HEADS * HEAD_DIM
A_QKV_W = A_Q_W + 2 * A_KV_W
B_GROUP_W = B_HEADS_PER_GROUP * HEAD_DIM
SLABS = B_GROUP_W // LANES
FOLD_STRIDE = 4
IN_ROWS = 1024
ROW_TILE = 512
SUB_ROWS = 256
ATTN_ROWS = 8192
VMEM_LIMIT = 56 * 1024 * 1024
LOG2E = math.log2(math.e)
Q_SCALE = HEAD_DIM ** -0.5 * LOG2E

F32 = jnp.float32
BF16 = jnp.bfloat16


def _sigmoid(x):
    return 0.5 * jnp.tanh(0.5 * x) + 0.5


def _layer_norm(y, g, b):
    mu = jnp.mean(y, axis=-1, keepdims=True)
    yc = y - mu
    var = jnp.mean(yc * yc, axis=-1, keepdims=True)
    return yc * lax.rsqrt(var + LN_EPS) * g + b


def _resident(layer, shape):
    return pl.BlockSpec((None,) + tuple(shape), lambda *_: (layer,) + (0,) * len(shape),
                        pipeline_mode=pl.Buffered(1))


def _mod_spec(layer, tiles_per_seq, d):
    return pl.BlockSpec((None, 1, 6, d), lambda i: (layer, i // tiles_per_seq, 0, 0))


def _folded_spec(dilation, rows, tiles_per_seq, width=B_GROUP_W):
    return pl.BlockSpec((1, dilation, rows // dilation, width),
                        lambda i: (i // tiles_per_seq, 0, i % tiles_per_seq, 0))


def _mod_kernel(ct_ref, w_ref, b_ref, o_ref):
    w = w_ref[0]
    for b in range(o_ref.shape[1]):
        c = ct_ref[:, b:b + 1]
        sc = c * _sigmoid(c)
        o_ref[0, b:b + 1, :] = jnp.sum(sc * w, axis=0, keepdims=True) + b_ref[0]


def _modulation(c, w_ada, b_ada):
    depth, d, n = w_ada.shape
    bsz = c.shape[0]
    ct = jnp.pad(c.T, ((0, 0), (0, LANES - bsz)))
    cols = 2 * d
    out = pl.pallas_call(
        _mod_kernel,
        grid=(depth, n // cols),
        in_specs=[pl.BlockSpec((d, LANES), lambda l, j: (0, 0)),
                  pl.BlockSpec((1, d, cols), lambda l, j: (l, 0, j)),
                  pl.BlockSpec((1, 1, cols), lambda l, j: (l, 0, j))],
        out_specs=pl.BlockSpec((1, bsz, cols), lambda l, j: (l, 0, j)),
        out_shape=jax.ShapeDtypeStruct((depth, bsz, n), F32),
        compiler_params=pltpu.CompilerParams(
            dimension_semantics=("arbitrary", "arbitrary"), vmem_limit_bytes=VMEM_LIMIT),
        name="adaln_mod",
    )(ct, w_ada, b_ada.reshape(depth, 1, n))
    return out.reshape(depth, bsz, n // d, d)


def _inproj_kernel(x_ref, mod_ref, w_ref, wqa_ref, qkva_ref, *rest):
    n_g = len(B_GROUPS)
    b_refs, (gates_ref, fold_ref, tmp_ref) = rest[:n_g], rest[n_g:]
    sh1 = mod_ref[0, 0:1, :]
    s1 = mod_ref[0, 1:2, :]
    u = (x_ref[...] * (1.0 + s1) + sh1).astype(BF16)

    def proj(off, width):
        return jnp.dot(u, w_ref[:, off:off + width], preferred_element_type=F32)

    d_model = gates_ref.shape[1] // 2
    kv_off = A_Q_W
    qkv_off = A_Q_W + 2 * A_KV_W
    gate_off = qkv_off + 3 * n_g * B_GROUP_W
    gates_ref[:, :d_model] = _sigmoid(proj(gate_off, d_model)).astype(BF16)
    gates_ref[:, d_model:] = _sigmoid(proj(gate_off + d_model, d_model)).astype(BF16)
    slot = slot2 = 0
    for g, (_, dilation) in enumerate(B_GROUPS):
        ref = b_refs[g]
        for t in range(3):
            r = proj(qkv_off + (t * n_g + g) * B_GROUP_W, B_GROUP_W)
            if t == 0:
                r = r * Q_SCALE
            c0 = t * B_GROUP_W
            if dilation == 1:
                ref[0, 0, :, c0:c0 + B_GROUP_W] = r.astype(BF16)
                continue
            for s in range(SLABS):
                fold_ref[slot, s] = r[:, s * LANES:(s + 1) * LANES]
            if dilation == FOLD_STRIDE:
                for rr in range(dilation):
                    for s in range(SLABS):
                        part = fold_ref[slot, s, pl.ds(rr, IN_ROWS // dilation, stride=dilation), :]
                        ref[0, rr, :, c0 + s * LANES:c0 + (s + 1) * LANES] = part.astype(BF16)
            else:
                assert dilation == FOLD_STRIDE * FOLD_STRIDE
                for s in range(SLABS):
                    for r1 in range(FOLD_STRIDE):
                        tmp_ref[slot2, s, r1] = fold_ref[
                            slot, s, pl.ds(r1, IN_ROWS // FOLD_STRIDE, stride=FOLD_STRIDE), :]
                for r1 in range(FOLD_STRIDE):
                    for r2 in range(FOLD_STRIDE):
                        for s in range(SLABS):
                            part = tmp_ref[slot2, s, r1,
                                           pl.ds(r2, IN_ROWS // dilation, stride=FOLD_STRIDE), :]
                            ref[0, r1 + FOLD_STRIDE * r2, :,
                                c0 + s * LANES:c0 + (s + 1) * LANES] = part.astype(BF16)
                slot2 += 1
            slot += 1
    qa = jnp.dot(u, wqa_ref[...], preferred_element_type=F32)
    qkva_ref[:, :A_Q_W] = (qa * Q_SCALE).astype(BF16)
    qkva_ref[:, A_Q_W:] = proj(kv_off, 2 * A_KV_W).astype(BF16)


def _inproj(x2, mod, w_in_b, wqa_b, layer, bsz, seq):
    t, d = x2.shape
    n = w_in_b.shape[2]
    tiles_per_seq = seq // IN_ROWS
    row = lambda w: pl.BlockSpec((IN_ROWS, w), lambda i: (i, 0))
    out_shapes = [jax.ShapeDtypeStruct((t, A_QKV_W), BF16)]
    out_specs = [row(A_QKV_W)]
    n_folded = n_two_pass = 0
    for _, dilation in B_GROUPS:
        out_shapes.append(jax.ShapeDtypeStruct((bsz, dilation, seq // dilation, 3 * B_GROUP_W), BF16))
        out_specs.append(_folded_spec(dilation, IN_ROWS, tiles_per_seq, 3 * B_GROUP_W))
        n_folded += 3 * (dilation > 1)
        n_two_pass += 3 * (dilation > FOLD_STRIDE)
    out_shapes.append(jax.ShapeDtypeStruct((t, 2 * d), BF16))
    out_specs.append(row(2 * d))
    return pl.pallas_call(
        _inproj_kernel,
        grid=(t // IN_ROWS,),
        in_specs=[row(d), _mod_spec(layer, tiles_per_seq, d),
                  _resident(layer, (d, n)), _resident(layer, (d, A_Q_W))],
        out_specs=out_specs,
        out_shape=out_shapes,
        scratch_shapes=[pltpu.VMEM((n_folded, SLABS, IN_ROWS, LANES), F32),
                        pltpu.VMEM((n_two_pass, SLABS, FOLD_STRIDE, IN_ROWS // FOLD_STRIDE, LANES),
                                   F32)],
        compiler_params=pltpu.CompilerParams(
            dimension_semantics=("arbitrary",), vmem_limit_bytes=VMEM_LIMIT),
        name="inproj",
    )(x2, mod, w_in_b, wqa_b)


def _fill_bias_tables(tab_ref, slopes_ref, head_ids, max_dist, stride, sinks_ref=None):
    qi = lax.broadcasted_iota(jnp.int32, (BLOCK, 2 * BLOCK), 0)
    sj = lax.broadcasted_iota(jnp.int32, (BLOCK, 2 * BLOCK), 1)
    dist = qi + BLOCK - sj
    valid = (dist >= 0) & (dist <= max_dist)
    valid_first = valid & (sj >= BLOCK)
    dist_f = (dist * stride).astype(F32)
    if sinks_ref is not None:
        assert max_dist < BLOCK
    for h, hid in enumerate(head_ids):
        bias = -(slopes_ref[hid] * dist_f) * LOG2E
        for v, ok in enumerate((valid, valid_first)):
            tab = jnp.where(ok, bias, NEG_INF)
            if sinks_ref is not None:
                tab = jnp.where(sj == 0, sinks_ref[hid] * LOG2E, tab)
            tab_ref[v, h * BLOCK:(h + 1) * BLOCK, :] = tab


def _attend(qs, k2, v2, tab):
    s = lax.dot_general(qs, k2, (((1,), (1,)), ((), ())), preferred_element_type=F32) + tab
    m = jnp.max(s, axis=-1, keepdims=True)
    e = jnp.exp2(s - m).astype(BF16)
    vext = jnp.concatenate([v2, jnp.ones((2 * BLOCK, LANES), BF16)], axis=1)
    acc = jnp.dot(e, vext, preferred_element_type=F32)
    return acc[:, :LANES], acc[:, LANES:], m


def _lane_masks():
    lane = lax.broadcasted_iota(jnp.int32, (1, LANES), 1)
    lo = lane < HEAD_DIM
    return lo, lo.astype(BF16), (~lo).astype(BF16)


def _prev_cur(prev_ref, cur_ref, i, cols, sq=0):
    if i == 0:
        return jnp.concatenate([prev_ref[sq, :, cols], cur_ref[sq, 0:BLOCK, cols]], axis=0)
    return cur_ref[sq, (i - 1) * BLOCK:(i + 1) * BLOCK, cols]


def _attn_a_kernel(slopes_ref, sinks_ref, q_ref, kvc_ref, kvp_ref, o_ref, tab_ref, *, n_blocks):
    b, j = pl.program_id(0), pl.program_id(1)
    group = A_Q_HEADS // A_KV_HEADS
    head_order = tuple(h for p in range(group) for h in (p, p + group))

    @pl.when((b == 0) & (j == 0))
    def _():
        _fill_bias_tables(tab_ref, slopes_ref, head_order, A_WINDOW - 1, 1, sinks_ref)

    lo, mask_lo, mask_hi = _lane_masks()
    first = jnp.where(j == 0, 1, 0)
    not_sink_row = lax.broadcasted_iota(jnp.int32, (2 * BLOCK, 1), 0) > 0
    for i in range(n_blocks):
        rows = slice(i * BLOCK, (i + 1) * BLOCK)
        kv = _prev_cur(kvp_ref, kvc_ref, i, slice(None))
        kv = jnp.where(not_sink_row, kv, jnp.zeros_like(kv))
        for p in range(group):
            cols = slice(p * LANES, (p + 1) * LANES)
            trows = slice(2 * p * BLOCK, (2 * p + 2) * BLOCK)
            q2 = q_ref[0, rows, cols]
            qs = jnp.concatenate([q2 * mask_lo, q2 * mask_hi], axis=0)
            tab = tab_ref[first, trows, :] if i == 0 else tab_ref[0, trows, :]
            acc, den, _ = _attend(qs, kv[:, :LANES], kv[:, LANES:], tab)
            acc, den = (jnp.where(lo, t[:BLOCK], t[BLOCK:]) for t in (acc, den))
            o_ref[0, rows, cols] = (acc / den).astype(BF16)


def _attention_a(slopes, sinks_l, qkva):
    bsz, seq, _ = qkva.shape
    qb = ATTN_ROWS // BLOCK
    kernel = functools.partial(_attn_a_kernel, n_blocks=qb)
    smem = pl.BlockSpec(memory_space=pltpu.SMEM)
    kv_block = A_Q_W // (2 * A_KV_W)
    return pl.pallas_call(
        kernel,
        grid=(bsz, seq // ATTN_ROWS),
        in_specs=[smem, smem,
                  pl.BlockSpec((1, ATTN_ROWS, A_Q_W), lambda b, j: (b, j, 0)),
                  pl.BlockSpec((1, ATTN_ROWS, 2 * A_KV_W), lambda b, j: (b, j, kv_block)),
                  pl.BlockSpec((1, BLOCK, 2 * A_KV_W),
                               lambda b, j: (b, jnp.maximum(j * qb - 1, 0), kv_block))],
        out_specs=pl.BlockSpec((1, ATTN_ROWS, A_Q_W), lambda b, j: (b, j, 0)),
        out_shape=jax.ShapeDtypeStruct((bsz, seq, A_Q_W), BF16),
        scratch_shapes=[pltpu.VMEM((2, A_Q_HEADS * BLOCK, 2 * BLOCK), F32)],
        compiler_params=pltpu.CompilerParams(
            dimension_semantics=("arbitrary", "arbitrary"), vmem_limit_bytes=VMEM_LIMIT),
        name="attn_swa",
    )(slopes, sinks_l, qkva, qkva, qkva)


def _attn_b_kernel(slopes_ref, q_ref, kc_ref, kp_ref, vc_ref, vp_ref, o_ref, lse_ref, tab_ref,
                   *, n_seqs, n_blocks, dilation, max_dist, head0):
    b, j = pl.program_id(0), pl.program_id(1)

    @pl.when((b == 0) & (j == 0))
    def _():
        heads = tuple(range(head0, head0 + B_HEADS_PER_GROUP))
        _fill_bias_tables(tab_ref, slopes_ref, heads, max_dist, dilation)

    lo, mask_lo, mask_hi = _lane_masks()
    first = jnp.where(j == 0, 1, 0)
    for sq in range(n_seqs):
        for i in range(n_blocks):
            rows = slice(i * BLOCK, (i + 1) * BLOCK)
            for p in range(B_HEADS_PER_GROUP // 2):
                cols = slice(p * LANES, (p + 1) * LANES)
                trows = slice(2 * p * BLOCK, (2 * p + 2) * BLOCK)
                q2 = q_ref[sq, rows, cols]
                qs = jnp.concatenate([q2 * mask_lo, q2 * mask_hi], axis=0)
                tab = tab_ref[first, trows, :] if i == 0 else tab_ref[0, trows, :]
                acc, den, m = _attend(qs, _prev_cur(kp_ref, kc_ref, i, cols, sq),
                                      _prev_cur(vp_ref, vc_ref, i, cols, sq), tab)
                acc, den, m = (jnp.where(lo, t[:BLOCK], t[BLOCK:]) for t in (acc, den, m))
                o_ref[sq, rows, cols] = (acc / den).astype(BF16)
                lse_ref[sq, rows, cols] = m + jnp.log(den) * LOG2E


def _attention_b(slopes, qkv, group):
    window, dilation = B_GROUPS[group]
    bsz, _, n, _ = qkv.shape
    w = B_GROUP_W
    rows = min(ATTN_ROWS, n)
    qb = rows // BLOCK
    n_seqs = ATTN_ROWS // rows
    assert (bsz * dilation) % n_seqs == 0
    qkv = qkv.reshape(bsz * dilation, n, 3 * w)
    kernel = functools.partial(
        _attn_b_kernel, n_seqs=n_seqs, n_blocks=qb, dilation=dilation,
        max_dist=window // dilation, head0=A_Q_HEADS + group * B_HEADS_PER_GROUP)
    cur = lambda col: pl.BlockSpec((n_seqs, rows, w), lambda b, j: (b, j, col))
    prev = lambda col: pl.BlockSpec((n_seqs, BLOCK, w),
                                    lambda b, j: (b, jnp.maximum(j * qb - 1, 0), col))
    o, lse = pl.pallas_call(
        kernel,
        grid=(bsz * dilation // n_seqs, n // rows),
        in_specs=[pl.BlockSpec(memory_space=pltpu.SMEM), cur(0), cur(1), prev(1), cur(2), prev(2)],
        out_specs=[cur(0), cur(0)],
        out_shape=[jax.ShapeDtypeStruct((bsz * dilation, n, w), BF16),
                   jax.ShapeDtypeStruct((bsz * dilation, n, w), F32)],
        scratch_shapes=[pltpu.VMEM((2, B_HEADS_PER_GROUP * BLOCK, 2 * BLOCK), F32)],
        compiler_params=pltpu.CompilerParams(
            dimension_semantics=("arbitrary", "arbitrary"), vmem_limit_bytes=VMEM_LIMIT),
        name=f"attn_dil{dilation}",
    )(slopes, qkv, qkv, qkv, qkv, qkv)
    return o.reshape(bsz, dilation, n, w), lse.reshape(bsz, dilation, n, w)


def _unfold(src_ref, slab_ref, tmp_ref, slot, dilation, r0):
    if dilation == 1:
        return src_ref[0, 0, r0:r0 + SUB_ROWS, :].astype(F32)
    n = SUB_ROWS // dilation
    f0 = r0 // dilation
    if dilation == FOLD_STRIDE:
        for rr in range(dilation):
            for s in range(SLABS):
                slab_ref[slot, s, pl.ds(r0 + rr, n, stride=dilation), :] = (
                    src_ref[0, rr, f0:f0 + n, s * LANES:(s + 1) * LANES].astype(F32))
    else:
        assert dilation == FOLD_STRIDE * FOLD_STRIDE
        for s in range(SLABS):
            for r1 in range(FOLD_STRIDE):
                for r2 in range(FOLD_STRIDE):
                    tmp_ref[slot % 2, s, r1, pl.ds(r2, n, stride=FOLD_STRIDE), :] = (
                        src_ref[0, r1 + FOLD_STRIDE * r2, f0:f0 + n,
                                s * LANES:(s + 1) * LANES].astype(F32))
                slab_ref[slot, s, pl.ds(r0 + r1, SUB_ROWS // FOLD_STRIDE, stride=FOLD_STRIDE), :] = (
                    tmp_ref[slot % 2, s, r1])
    return jnp.concatenate([slab_ref[slot, s, r0:r0 + SUB_ROWS, :] for s in range(SLABS)],
                           axis=1)


def _post_kernel(x_ref, mod_ref, ya_ref, o0_ref, o1_ref, o2_ref, l0_ref, l1_ref, l2_ref,
                 gates_ref, wa_ref, wb_ref, wo_ref, wg_ref, wu_ref, wd_ref, ln_ref,
                 out_ref, h_ref, slab_ref, x1_ref, act_ref, tmp_ref, *, alpha, ff_chunk):
    d_model = x_ref.shape[1]
    g1 = mod_ref[0, 2:3, :]
    sh2 = mod_ref[0, 3:4, :]
    s2 = mod_ref[0, 4:5, :]
    g2 = mod_ref[0, 5:6, :]
    d_ff = wg_ref.shape[1]
    held = {}

    def mix(r0):
        slot = 0
        os_, ls_ = [], []
        for (_, dilation), o_ref, l_ref in zip(B_GROUPS, (o0_ref, o1_ref, o2_ref),
                                               (l0_ref, l1_ref, l2_ref)):
            os_.append(_unfold(o_ref, slab_ref, tmp_ref, slot, dilation, r0))
            ls_.append(_unfold(l_ref, slab_ref, tmp_ref, slot + 1, dilation, r0))
            slot += 2 * (dilation > 1)
        lmax = jnp.maximum(jnp.maximum(ls_[0], ls_[1]), ls_[2])
        es = [jnp.exp2(l - lmax) for l in ls_]
        yb = (es[0] * os_[0] + es[1] * os_[1] + es[2] * os_[2]) / (es[0] + es[1] + es[2])
        held[r0, "yb"] = yb.astype(BF16)

    def merge(r0):
        rs = slice(r0, r0 + SUB_ROWS)
        a = jnp.dot(ya_ref[rs, :], wa_ref[...], preferred_element_type=F32)
        bb = jnp.dot(held[r0, "yb"], wb_ref[...], preferred_element_type=F32)
        merged = (gates_ref[rs, :d_model].astype(F32) * a
                  + gates_ref[rs, d_model:].astype(F32) * bb)
        act_ref[rs, :] = merged.astype(BF16)

    def out_proj(r0):
        rs = slice(r0, r0 + SUB_ROWS)
        held[r0, "t1"] = jnp.dot(act_ref[rs, :], wo_ref[...], preferred_element_type=F32)

    def norm1(r0):
        rs = slice(r0, r0 + SUB_ROWS)
        x1 = _layer_norm(alpha * x_ref[rs, :] + g1 * held[r0, "t1"],
                         ln_ref[0:1, :], ln_ref[1:2, :])
        x1_ref[rs, :] = alpha * x1
        act_ref[rs, :] = (x1 * (1.0 + s2) + sh2).astype(BF16)

    def ffn_up(r0, c):
        rs = slice(r0, r0 + SUB_ROWS)
        gate = jnp.dot(act_ref[rs, :], wg_ref[:, c:c + ff_chunk], preferred_element_type=F32)
        up = jnp.dot(act_ref[rs, :], wu_ref[:, c:c + ff_chunk], preferred_element_type=F32)
        h_ref[rs, c:c + ff_chunk] = (gate * _sigmoid(gate) * up).astype(BF16)

    def ffn_down(r0):
        rs = slice(r0, r0 + SUB_ROWS)
        held[r0, "t2"] = jnp.dot(h_ref[rs, :], wd_ref[...], preferred_element_type=F32)

    def norm2(r0):
        rs = slice(r0, r0 + SUB_ROWS)
        out_ref[rs, :] = _layer_norm(x1_ref[rs, :] + g2 * held[r0, "t2"],
                                     ln_ref[2:3, :], ln_ref[3:4, :])

    chunks = list(range(0, d_ff, ff_chunk))
    third = -(-len(chunks) // 3)

    def ffn_up_part(k):
        return lambda r0: [ffn_up(r0, c) for c in chunks[k * third:(k + 1) * third]]

    phases = [mix, merge, out_proj, norm1, ffn_up_part(0), ffn_up_part(1), ffn_up_part(2),
              ffn_down, norm2]
    n_sub = ROW_TILE // SUB_ROWS
    for t in range(len(phases) + n_sub - 1):
        for k in range(n_sub):
            if 0 <= t - k < len(phases):
                phases[t - k](k * SUB_ROWS)


def _post(x2, mod, ya, obs, lses, gates, weights, layer, seq, alpha):
    t, d = x2.shape
    d_ff = weights[3].shape[2]
    tiles_per_seq = seq // ROW_TILE
    row = lambda w: pl.BlockSpec((ROW_TILE, w), lambda i: (i, 0))
    folded = [_folded_spec(dilation, ROW_TILE, tiles_per_seq) for _, dilation in B_GROUPS]
    n_slabs = 2 * sum(dilation > 1 for _, dilation in B_GROUPS)
    kernel = functools.partial(_post_kernel, alpha=alpha, ff_chunk=256)
    return pl.pallas_call(
        kernel,
        grid=(t // ROW_TILE,),
        in_specs=[row(d), _mod_spec(layer, tiles_per_seq, d),
                  row(A_Q_W)] + folded + folded + [row(2 * d)] +
                 [_resident(layer, w.shape[1:]) for w in weights],
        out_specs=row(d),
        out_shape=jax.ShapeDtypeStruct((t, d), F32),
        scratch_shapes=[pltpu.VMEM((ROW_TILE, d_ff), BF16),
                        pltpu.VMEM((n_slabs, SLABS, ROW_TILE, LANES), F32),
                        pltpu.VMEM((ROW_TILE, d), F32),
                        pltpu.VMEM((ROW_TILE, d), BF16),
                        pltpu.VMEM((2, SLABS, FOLD_STRIDE, SUB_ROWS // FOLD_STRIDE, LANES), F32)],
        compiler_params=pltpu.CompilerParams(
            dimension_semantics=("arbitrary",), vmem_limit_bytes=VMEM_LIMIT),
        name="merge_ffn",
    )(x2, mod, ya, *obs, *lses, gates, *weights)


def _pair_order(w, axis):
    group = A_Q_HEADS // A_KV_HEADS
    shape = w.shape
    w = w.reshape(shape[:axis] + (A_KV_HEADS, group, HEAD_DIM) + shape[axis + 1:])
    return jnp.swapaxes(w, axis, axis + 1).reshape(shape)


def kernel(x, c, w_ada, b_ada, w_in, sinks, w_a, w_b, w_o, ln1_g, ln1_b,
           w_gate, w_up, w_down, ln2_g, ln2_b):
    bsz, seq, d = x.shape
    depth = w_ada.shape[0]
    alpha = (2 * depth) ** 0.25
    max_dilation = max(dil for _, dil in B_GROUPS)
    assert seq % ATTN_ROWS == 0 and seq % IN_ROWS == 0 and seq % ROW_TILE == 0
    assert seq % (max_dilation * BLOCK) == 0

    slopes = jnp.exp2(-8.0 * jnp.arange(1, N_ATTN_HEADS + 1, dtype=F32) / N_ATTN_HEADS)
    mod = _modulation(c, w_ada, b_ada)
    w_in_b = w_in.astype(BF16)
    wqa_b = _pair_order(w_in[:, :, :A_Q_W], 2).astype(BF16)
    post_weights = (_pair_order(w_a, 1).astype(BF16), w_b.astype(BF16), w_o.astype(BF16),
                    w_gate.astype(BF16), w_up.astype(BF16), w_down.astype(BF16),
                    jnp.stack([ln1_g, ln1_b, ln2_g, ln2_b], axis=1))
    x2 = x.reshape(bsz * seq, d)
    for l in range(depth):
        qkva, *qkv_groups, gates = _inproj(x2, mod, w_in_b, wqa_b, l, bsz, seq)
        ya = _attention_a(slopes, sinks[l], qkva.reshape(bsz, seq, -1))
        obs, lses = [], []
        for g, qkv in enumerate(qkv_groups):
            o, lse = _attention_b(slopes, qkv, g)
            obs.append(o)
            lses.append(lse)
        x2 = _post(x2, mod, ya.reshape(bsz * seq, -1), obs, lses, gates, post_weights, l,
                   seq, alpha)
    return x2.reshape(bsz, seq, d)
```

```python
import functools
import math

import jax
import jax.numpy as jnp
from jax import lax
from jax.experimental import pallas as pl
from jax.experimental.pallas import tpu as pltpu

HEAD_DIM = 64
A_Q_HEADS = 8
A_KV_HEADS = 2
A_WINDOW = 128
B_GROUPS = ((128, 1), (512, 4), (2048, 16))
B_HEADS_PER_GROUP = 4
N_ATTN_HEADS = A_Q_HEADS + B_HEADS_PER_GROUP * len(B_GROUPS)
BLOCK = 128
LN_EPS = 1e-5
NEG_INF = -1e30

LANES = 128
A_Q_W = A_Q_HEADS * HEAD_DIM
A_KV_W = A_KV_HEADS * HEAD_DIM
A_QKV_W = A_Q_W + 2 * A_KV_W
B_GROUP_W = B_HEADS_PER_GROUP * HEAD_DIM
SLABS = B_GROUP_W // LANES
FOLD_STRIDE = 4
IN_ROWS = 1024
ROW_TILE = 512
SUB_ROWS = 256
ATTN_ROWS = 4096
VMEM_LIMIT = 56 * 1024 * 1024
LOG2E = math.log2(math.e)
Q_SCALE = HEAD_DIM ** -0.5 * LOG2E

F32 = jnp.float32
BF16 = jnp.bfloat16


def _sigmoid(x):
    return 0.5 * jnp.tanh(0.5 * x) + 0.5


def _layer_norm(y, g, b):
    mu = jnp.mean(y, axis=-1, keepdims=True)
    yc = y - mu
    var = jnp.mean(yc * yc, axis=-1, keepdims=True)
    return yc * lax.rsqrt(var + LN_EPS) * g + b


def _resident(layer, shape):
    return pl.BlockSpec((None,) + tuple(shape), lambda *_: (layer,) + (0,) * len(shape),
                        pipeline_mode=pl.Buffered(1))


def _mod_spec(layer, tiles_per_seq, d):
    return pl.BlockSpec((None, 1, 6, d), lambda i: (layer, i // tiles_per_seq, 0, 0))


def _folded_spec(dilation, rows, tiles_per_seq, width=B_GROUP_W):
    return pl.BlockSpec((1, dilation, rows // dilation, width),
                        lambda i: (i // tiles_per_seq, 0, i % tiles_per_seq, 0))


def _mod_kernel(ct_ref, w_ref, b_ref, o_ref):
    w = w_ref[0]
    for b in range(o_ref.shape[1]):
        c = ct_ref[:, b:b + 1]
        sc = c * _sigmoid(c)
        o_ref[0, b:b + 1, :] = jnp.sum(sc * w, axis=0, keepdims=True) + b_ref[0]


def _modulation(c, w_ada, b_ada):
    depth, d, n = w_ada.shape
    bsz = c.shape[0]
    ct = jnp.pad(c.T, ((0, 0), (0, LANES - bsz)))
    cols = 2 * d
    out = pl.pallas_call(
        _mod_kernel,
        grid=(depth, n // cols),
        in_specs=[pl.BlockSpec((d, LANES), lambda l, j: (0, 0)),
                  pl.BlockSpec((1, d, cols), lambda l, j: (l, 0, j)),
                  pl.BlockSpec((1, 1, cols), lambda l, j: (l, 0, j))],
        out_specs=pl.BlockSpec((1, bsz, cols), lambda l, j: (l, 0, j)),
        out_shape=jax.ShapeDtypeStruct((depth, bsz, n), F32),
        compiler_params=pltpu.CompilerParams(
            dimension_semantics=("arbitrary", "arbitrary"), vmem_limit_bytes=VMEM_LIMIT),
        name="adaln_mod",
    )(ct, w_ada, b_ada.reshape(depth, 1, n))
    return out.reshape(depth, bsz, n // d, d)


def _inproj_kernel(x_ref, mod_ref, w_ref, wqa_ref, qkva_ref, *rest):
    n_g = len(B_GROUPS)
    b_refs, (gates_ref, u_ref, fold_ref, tmp_ref) = rest[:n_g], rest[n_g:]
    sh1 = mod_ref[0, 0:1, :]
    s1 = mod_ref[0, 1:2, :]
    u_ref[...] = (x_ref[...] * (1.0 + s1) + sh1).astype(BF16)

    def proj(off, width):
        return jnp.dot(u_ref[...], w_ref[:, off:off + width], preferred_element_type=F32)

    d_model = gates_ref.shape[1] // 2
    kv_off = A_Q_W
    qkv_off = A_Q_W + 2 * A_KV_W
    gate_off = qkv_off + 3 * n_g * B_GROUP_W
    gates_ref[:, :d_model] = _sigmoid(proj(gate_off, d_model)).astype(BF16)
    gates_ref[:, d_model:] = _sigmoid(proj(gate_off + d_model, d_model)).astype(BF16)
    slot = slot2 = 0
    for g, (_, dilation) in enumerate(B_GROUPS):
        ref = b_refs[g]
        for t in range(3):
            r = proj(qkv_off + (t * n_g + g) * B_GROUP_W, B_GROUP_W)
            if t == 0:
                r = r * Q_SCALE
            c0 = t * B_GROUP_W
            if dilation == 1:
                ref[0, 0, :, c0:c0 + B_GROUP_W] = r.astype(BF16)
                continue
            for s in range(SLABS):
                fold_ref[slot, s] = r[:, s * LANES:(s + 1) * LANES]
            if dilation == FOLD_STRIDE:
                for rr in range(dilation):
                    for s in range(SLABS):
                        part = fold_ref[slot, s, pl.ds(rr, IN_ROWS // dilation, stride=dilation), :]
                        ref[0, rr, :, c0 + s * LANES:c0 + (s + 1) * LANES] = part.astype(BF16)
            else:
                assert dilation == FOLD_STRIDE * FOLD_STRIDE
                for s in range(SLABS):
                    for r1 in range(FOLD_STRIDE):
                        tmp_ref[slot2, s, r1] = fold_ref[
                            slot, s, pl.ds(r1, IN_ROWS // FOLD_STRIDE, stride=FOLD_STRIDE), :]
                for r1 in range(FOLD_STRIDE):
                    for r2 in range(FOLD_STRIDE):
                        for s in range(SLABS):
                            part = tmp_ref[slot2, s, r1,
                                           pl.ds(r2, IN_ROWS // dilation, stride=FOLD_STRIDE), :]
                            ref[0, r1 + FOLD_STRIDE * r2, :,
                                c0 + s * LANES:c0 + (s + 1) * LANES] = part.astype(BF16)
                slot2 += 1
            slot += 1
    qa = jnp.dot(u_ref[...], wqa_ref[...], preferred_element_type=F32)
    qkva_ref[:, :A_Q_W] = (qa * Q_SCALE).astype(BF16)
    qkva_ref[:, A_Q_W:] = proj(kv_off, 2 * A_KV_W).astype(BF16)


def _inproj(x2, mod, w_in_b, wqa_b, layer, bsz, seq):
    t, d = x2.shape
    n = w_in_b.shape[2]
    tiles_per_seq = seq // IN_ROWS
    row = lambda w: pl.BlockSpec((IN_ROWS, w), lambda i: (i, 0))
    out_shapes = [jax.ShapeDtypeStruct((t, A_QKV_W), BF16)]
    out_specs = [row(A_QKV_W)]
    n_folded = n_two_pass = 0
    for _, dilation in B_GROUPS:
        out_shapes.append(jax.ShapeDtypeStruct((bsz, dilation, seq // dilation, 3 * B_GROUP_W), BF16))
        out_specs.append(_folded_spec(dilation, IN_ROWS, tiles_per_seq, 3 * B_GROUP_W))
        n_folded += 3 * (dilation > 1)
        n_two_pass += 3 * (dilation > FOLD_STRIDE)
    out_shapes.append(jax.ShapeDtypeStruct((t, 2 * d), BF16))
    out_specs.append(row(2 * d))
    return pl.pallas_call(
        _inproj_kernel,
        grid=(t // IN_ROWS,),
        in_specs=[row(d), _mod_spec(layer, tiles_per_seq, d),
                  _resident(layer, (d, n)), _resident(layer, (d, A_Q_W))],
        out_specs=out_specs,
        out_shape=out_shapes,
        scratch_shapes=[pltpu.VMEM((IN_ROWS, d), BF16),
                        pltpu.VMEM((n_folded, SLABS, IN_ROWS, LANES), F32),
                        pltpu.VMEM((n_two_pass, SLABS, FOLD_STRIDE, IN_ROWS // FOLD_STRIDE, LANES),
                                   F32)],
        compiler_params=pltpu.CompilerParams(
            dimension_semantics=("arbitrary",), vmem_limit_bytes=VMEM_LIMIT,
            allow_input_fusion=[False, False, True, True]),
        name="inproj",
    )(x2, mod, w_in_b, wqa_b)


def _fill_bias_tables(tab_ref, slopes_ref, head_ids, max_dist, stride, sinks_ref=None):
    qi = lax.broadcasted_iota(jnp.int32, (BLOCK, 2 * BLOCK), 0)
    sj = lax.broadcasted_iota(jnp.int32, (BLOCK, 2 * BLOCK), 1)
    dist = qi + BLOCK - sj
    valid = (dist >= 0) & (dist <= max_dist)
    valid_first = valid & (sj >= BLOCK)
    dist_f = (dist * stride).astype(F32)
    if sinks_ref is not None:
        assert max_dist < BLOCK
    for h, hid in enumerate(head_ids):
        bias = -(slopes_ref[hid] * dist_f) * LOG2E
        for v, ok in enumerate((valid, valid_first)):
            tab = jnp.where(ok, bias, NEG_INF)
            if sinks_ref is not None:
                tab = jnp.where(sj == 0, sinks_ref[hid] * LOG2E, tab)
            tab_ref[v, h * BLOCK:(h + 1) * BLOCK, :] = tab


def _attend(qs, k2, v2, tab):
    s = lax.dot_general(qs, k2, (((1,), (1,)), ((), ())), preferred_element_type=F32) + tab
    m = jnp.max(s, axis=-1, keepdims=True)
    e = jnp.exp2(s - m).astype(BF16)
    vext = jnp.concatenate([v2, jnp.ones((2 * BLOCK, LANES), BF16)], axis=1)
    acc = jnp.dot(e, vext, preferred_element_type=F32)
    return acc[:, :LANES], acc[:, LANES:], m


def _lane_masks():
    lane = lax.broadcasted_iota(jnp.int32, (1, LANES), 1)
    lo = lane < HEAD_DIM
    return lo, lo.astype(BF16), (~lo).astype(BF16)


def _prev_cur(prev_ref, cur_ref, i, cols, sq=0):
    if i == 0:
        return jnp.concatenate([prev_ref[sq, :, cols], cur_ref[sq, 0:BLOCK, cols]], axis=0)
    return cur_ref[sq, (i - 1) * BLOCK:(i + 1) * BLOCK, cols]


def _attn_a_kernel(slopes_ref, sinks_ref, q_ref, kvc_ref, kvp_ref, o_ref, tab_ref, *, n_blocks):
    b, j = pl.program_id(0), pl.program_id(1)
    group = A_Q_HEADS // A_KV_HEADS
    head_order = tuple(h for p in range(group) for h in (p, p + group))

    @pl.when((b == 0) & (j == 0))
    def _():
        _fill_bias_tables(tab_ref, slopes_ref, head_order, A_WINDOW - 1, 1, sinks_ref)

    lo, mask_lo, mask_hi = _lane_masks()
    first = jnp.where(j == 0, 1, 0)
    not_sink_row = lax.broadcasted_iota(jnp.int32, (2 * BLOCK, 1), 0) > 0
    for i in range(n_blocks):
        rows = slice(i * BLOCK, (i + 1) * BLOCK)
        kv = _prev_cur(kvp_ref, kvc_ref, i, slice(None))
        kv = jnp.where(not_sink_row, kv, jnp.zeros_like(kv))
        for p in range(group):
            cols = slice(p * LANES, (p + 1) * LANES)
            trows = slice(2 * p * BLOCK, (2 * p + 2) * BLOCK)
            q2 = q_ref[0, rows, cols]
            qs = jnp.concatenate([q2 * mask_lo, q2 * mask_hi], axis=0)
            tab = tab_ref[first, trows, :] if i == 0 else tab_ref[0, trows, :]
            acc, den, _ = _attend(qs, kv[:, :LANES], kv[:, LANES:], tab)
            acc, den = (jnp.where(lo, t[:BLOCK], t[BLOCK:]) for t in (acc, den))
            o_ref[0, rows, cols] = (acc / den).astype(BF16)


def _attention_a(slopes, sinks_l, qkva):
    bsz, seq, _ = qkva.shape
    qb = ATTN_ROWS // BLOCK
    kernel = functools.partial(_attn_a_kernel, n_blocks=qb)
    smem = pl.BlockSpec(memory_space=pltpu.SMEM)
    kv_block = A_Q_W // (2 * A_KV_W)
    return pl.pallas_call(
        kernel,
        grid=(bsz, seq // ATTN_ROWS),
        in_specs=[smem, smem,
                  pl.BlockSpec((1, ATTN_ROWS, A_Q_W), lambda b, j: (b, j, 0)),
                  pl.BlockSpec((1, ATTN_ROWS, 2 * A_KV_W), lambda b, j: (b, j, kv_block)),
                  pl.BlockSpec((1, BLOCK, 2 * A_KV_W),
                               lambda b, j: (b, jnp.maximum(j * qb - 1, 0), kv_block))],
        out_specs=pl.BlockSpec((1, ATTN_ROWS, A_Q_W), lambda b, j: (b, j, 0)),
        out_shape=jax.ShapeDtypeStruct((bsz, seq, A_Q_W), BF16),
        scratch_shapes=[pltpu.VMEM((2, A_Q_HEADS * BLOCK, 2 * BLOCK), F32)],
        compiler_params=pltpu.CompilerParams(
            dimension_semantics=("arbitrary", "arbitrary"), vmem_limit_bytes=VMEM_LIMIT),
        name="attn_swa",
    )(slopes, sinks_l, qkva, qkva, qkva)


def _attn_b_kernel(slopes_ref, q_ref, kc_ref, kp_ref, vc_ref, vp_ref, o_ref, lse_ref, tab_ref,
                   *, n_seqs, n_blocks, dilation, max_dist, head0):
    b, j = pl.program_id(0), pl.program_id(1)

    @pl.when((b == 0) & (j == 0))
    def _():
        heads = tuple(range(head0, head0 + B_HEADS_PER_GROUP))
        _fill_bias_tables(tab_ref, slopes_ref, heads, max_dist, dilation)

    lo, mask_lo, mask_hi = _lane_masks()
    first = jnp.where(j == 0, 1, 0)
    for sq in range(n_seqs):
        for i in range(n_blocks):
            rows = slice(i * BLOCK, (i + 1) * BLOCK)
            for p in range(B_HEADS_PER_GROUP // 2):
                cols = slice(p * LANES, (p + 1) * LANES)
                trows = slice(2 * p * BLOCK, (2 * p + 2) * BLOCK)
                q2 = q_ref[sq, rows, cols]
                qs = jnp.concatenate([q2 * mask_lo, q2 * mask_hi], axis=0)
                tab = tab_ref[first, trows, :] if i == 0 else tab_ref[0, trows, :]
                acc, den, m = _attend(qs, _prev_cur(kp_ref, kc_ref, i, cols, sq),
                                      _prev_cur(vp_ref, vc_ref, i, cols, sq), tab)
                acc, den, m = (jnp.where(lo, t[:BLOCK], t[BLOCK:]) for t in (acc, den, m))
                o_ref[sq, rows, cols] = (acc / den).astype(BF16)
                lse_ref[sq, rows, cols] = m + jnp.log(den) * LOG2E


def _attention_b(slopes, qkv, group):
    window, dilation = B_GROUPS[group]
    bsz, _, n, _ = qkv.shape
    w = B_GROUP_W
    rows = min(ATTN_ROWS, n)
    qb = rows // BLOCK
    n_seqs = ATTN_ROWS // rows
    assert (bsz * dilation) % n_seqs == 0
    qkv = qkv.reshape(bsz * dilation, n, 3 * w)
    kernel = functools.partial(
        _attn_b_kernel, n_seqs=n_seqs, n_blocks=qb, dilation=dilation,
        max_dist=window // dilation, head0=A_Q_HEADS + group * B_HEADS_PER_GROUP)
    cur = lambda col: pl.BlockSpec((n_seqs, rows, w), lambda b, j: (b, j, col))
    prev = lambda col: pl.BlockSpec((n_seqs, BLOCK, w),
                                    lambda b, j: (b, jnp.maximum(j * qb - 1, 0), col))
    o, lse = pl.pallas_call(
        kernel,
        grid=(bsz * dilation // n_seqs, n // rows),
        in_specs=[pl.BlockSpec(memory_space=pltpu.SMEM), cur(0), cur(1), prev(1), cur(2), prev(2)],
        out_specs=[cur(0), cur(0)],
        out_shape=[jax.ShapeDtypeStruct((bsz * dilation, n, w), BF16),
                   jax.ShapeDtypeStruct((bsz * dilation, n, w), F32)],
        scratch_shapes=[pltpu.VMEM((2, B_HEADS_PER_GROUP * BLOCK, 2 * BLOCK), F32)],
        compiler_params=pltpu.CompilerParams(
            dimension_semantics=("arbitrary", "arbitrary"), vmem_limit_bytes=VMEM_LIMIT),
        name=f"attn_dil{dilation}",
    )(slopes, qkv, qkv, qkv, qkv, qkv)
    return o.reshape(bsz, dilation, n, w), lse.reshape(bsz, dilation, n, w)


def _unfold(src_ref, slab_ref, tmp_ref, slot, dilation, r0):
    if dilation == 1:
        return src_ref[0, 0, r0:r0 + SUB_ROWS, :].astype(F32)
    n = SUB_ROWS // dilation
    f0 = r0 // dilation
    if dilation == FOLD_STRIDE:
        for rr in range(dilation):
            for s in range(SLABS):
                slab_ref[slot, s, pl.ds(r0 + rr, n, stride=dilation), :] = (
                    src_ref[0, rr, f0:f0 + n, s * LANES:(s + 1) * LANES].astype(F32))
    else:
        assert dilation == FOLD_STRIDE * FOLD_STRIDE
        for s in range(SLABS):
            for r1 in range(FOLD_STRIDE):
                for r2 in range(FOLD_STRIDE):
                    tmp_ref[slot % 2, s, r1, pl.ds(r2, n, stride=FOLD_STRIDE), :] = (
                        src_ref[0, r1 + FOLD_STRIDE * r2, f0:f0 + n,
                                s * LANES:(s + 1) * LANES].astype(F32))
                slab_ref[slot, s, pl.ds(r0 + r1, SUB_ROWS // FOLD_STRIDE, stride=FOLD_STRIDE), :] = (
                    tmp_ref[slot % 2, s, r1])
    return jnp.concatenate([slab_ref[slot, s, r0:r0 + SUB_ROWS, :] for s in range(SLABS)],
                           axis=1)


def _post_kernel(x_ref, mod_ref, ya_ref, o0_ref, o1_ref, o2_ref, l0_ref, l1_ref, l2_ref,
                 gates_ref, wa_ref, wb_ref, wo_ref, wg_ref, wu_ref, wd_ref, ln_ref,
                 out_ref, h_ref, slab_ref, x1_ref, act_ref, tmp_ref, *, alpha, ff_chunk):
    d_model = x_ref.shape[1]
    g1 = mod_ref[0, 2:3, :]
    sh2 = mod_ref[0, 3:4, :]
    s2 = mod_ref[0, 4:5, :]
    g2 = mod_ref[0, 5:6, :]
    d_ff = wg_ref.shape[1]
    held = {}

    def mix(r0):
        slot = 0
        os_, ls_ = [], []
        for (_, dilation), o_ref, l_ref in zip(B_GROUPS, (o0_ref, o1_ref, o2_ref),
                                               (l0_ref, l1_ref, l2_ref)):
            os_.append(_unfold(o_ref, slab_ref, tmp_ref, slot, dilation, r0))
            ls_.append(_unfold(l_ref, slab_ref, tmp_ref, slot + 1, dilation, r0))
            slot += 2 * (dilation > 1)
        lmax = jnp.maximum(jnp.maximum(ls_[0], ls_[1]), ls_[2])
        es = [jnp.exp2(l - lmax) for l in ls_]
        yb = (es[0] * os_[0] + es[1] * os_[1] + es[2] * os_[2]) / (es[0] + es[1] + es[2])
        held[r0, "yb"] = yb.astype(BF16)

    def merge(r0):
        rs = slice(r0, r0 + SUB_ROWS)
        a = jnp.dot(ya_ref[rs, :], wa_ref[...], preferred_element_type=F32)
        bb = jnp.dot(held[r0, "yb"], wb_ref[...], preferred_element_type=F32)
        merged = (gates_ref[rs, :d_model].astype(F32) * a
                  + gates_ref[rs, d_model:].astype(F32) * bb)
        act_ref[rs, :] = merged.astype(BF16)

    def out_proj(r0):
        rs = slice(r0, r0 + SUB_ROWS)
        held[r0, "t1"] = jnp.dot(act_ref[rs, :], wo_ref[...], preferred_element_type=F32)

    def norm1(r0):
        rs = slice(r0, r0 + SUB_ROWS)
        x1 = _layer_norm(alpha * x_ref[rs, :] + g1 * held[r0, "t1"],
                         ln_ref[0:1, :], ln_ref[1:2, :])
        x1_ref[rs, :] = x1
        act_ref[rs, :] = (x1 * (1.0 + s2) + sh2).astype(BF16)

    def ffn_up(r0, c):
        rs = slice(r0, r0 + SUB_ROWS)
        gate = jnp.dot(act_ref[rs, :], wg_ref[:, c:c + ff_chunk], preferred_element_type=F32)
        up = jnp.dot(act_ref[rs, :], wu_ref[:, c:c + ff_chunk], preferred_element_type=F32)
        h_ref[rs, c:c + ff_chunk] = (gate * _sigmoid(gate) * up).astype(BF16)

    def ffn_down(r0):
        rs = slice(r0, r0 + SUB_ROWS)
        held[r0, "t2"] = jnp.dot(h_ref[rs, :], wd_ref[...], preferred_element_type=F32)

    def norm2(r0):
        rs = slice(r0, r0 + SUB_ROWS)
        out_ref[rs, :] = _layer_norm(alpha * x1_ref[rs, :] + g2 * held[r0, "t2"],
                                     ln_ref[2:3, :], ln_ref[3:4, :])

    chunks = list(range(0, d_ff, ff_chunk))
    third = -(-len(chunks) // 3)

    def ffn_up_part(k):
        return lambda r0: [ffn_up(r0, c) for c in chunks[k * third:(k + 1) * third]]

    phases = [mix, merge, out_proj, norm1, ffn_up_part(0), ffn_up_part(1), ffn_up_part(2),
              ffn_down, norm2]
    n_sub = ROW_TILE // SUB_ROWS
    for t in range(len(phases) + n_sub - 1):
        for k in range(n_sub):
            if 0 <= t - k < len(phases):
                phases[t - k](k * SUB_ROWS)


def _post(x2, mod, ya, obs, lses, gates, weights, layer, seq, alpha):
    t, d = x2.shape
    d_ff = weights[3].shape[2]
    tiles_per_seq = seq // ROW_TILE
    row = lambda w: pl.BlockSpec((ROW_TILE, w), lambda i: (i, 0))
    folded = [_folded_spec(dilation, ROW_TILE, tiles_per_seq) for _, dilation in B_GROUPS]
    n_slabs = 2 * sum(dilation > 1 for _, dilation in B_GROUPS)
    kernel = functools.partial(_post_kernel, alpha=alpha, ff_chunk=256)
    return pl.pallas_call(
        kernel,
        grid=(t // ROW_TILE,),
        in_specs=[row(d), _mod_spec(layer, tiles_per_seq, d),
                  row(A_Q_W)] + folded + folded + [row(2 * d)] +
                 [_resident(layer, w.shape[1:]) for w in weights],
        out_specs=row(d),
        out_shape=jax.ShapeDtypeStruct((t, d), F32),
        scratch_shapes=[pltpu.VMEM((ROW_TILE, d_ff), BF16),
                        pltpu.VMEM((n_slabs, SLABS, ROW_TILE, LANES), F32),
                        pltpu.VMEM((ROW_TILE, d), F32),
                        pltpu.VMEM((ROW_TILE, d), BF16),
                        pltpu.VMEM((2, SLABS, FOLD_STRIDE, SUB_ROWS // FOLD_STRIDE, LANES), F32)],
        compiler_params=pltpu.CompilerParams(
            dimension_semantics=("arbitrary",), vmem_limit_bytes=VMEM_LIMIT,
            allow_input_fusion=[False] * (4 + 2 * len(B_GROUPS)) + [True] * 6 + [False]),
        name="merge_ffn",
    )(x2, mod, ya, *obs, *lses, gates, *weights)


def _pair_order(w, axis):
    group = A_Q_HEADS // A_KV_HEADS
    shape = w.shape
    w = w.reshape(shape[:axis] + (A_KV_HEADS, group, HEAD_DIM) + shape[axis + 1:])
    return jnp.swapaxes(w, axis, axis + 1).reshape(shape)


def kernel(x, c, w_ada, b_ada, w_in, sinks, w_a, w_b, w_o, ln1_g, ln1_b,
           w_gate, w_up, w_down, ln2_g, ln2_b):
    bsz, seq, d = x.shape
    depth = w_ada.shape[0]
    alpha = (2 * depth) ** 0.25
    max_dilation = max(dil for _, dil in B_GROUPS)
    assert seq % ATTN_ROWS == 0 and seq % IN_ROWS == 0 and seq % ROW_TILE == 0
    assert seq % (max_dilation * BLOCK) == 0

    slopes = jnp.exp2(-8.0 * jnp.arange(1, N_ATTN_HEADS + 1, dtype=F32) / N_ATTN_HEADS)
    mod = _modulation(c, w_ada, b_ada)
    w_in_b = w_in.astype(BF16)
    wqa_b = _pair_order(w_in[:, :, :A_Q_W], 2).astype(BF16)
    post_weights = (_pair_order(w_a, 1).astype(BF16), w_b.astype(BF16), w_o.astype(BF16),
                    w_gate.astype(BF16), w_up.astype(BF16), w_down.astype(BF16),
                    jnp.stack([ln1_g, ln1_b, ln2_g, ln2_b], axis=1))
    x2 = x.reshape(bsz * seq, d)
    for l in range(depth):
        qkva, *qkv_groups, gates = _inproj(x2, mod, w_in_b, wqa_b, l, bsz, seq)
        ya = _attention_a(slopes, sinks[l], qkva.reshape(bsz, seq, -1))
        obs, lses = [], []
        for g, qkv in enumerate(qkv_groups):
            o, lse = _attention_b(slopes, qkv, g)
            obs.append(o)
            lses.append(lse)
        x2 = _post(x2, mod, ya.reshape(bsz * seq, -1), obs, lses, gates, post_weights, l,
                   seq, alpha)
    return x2.reshape(bsz, seq, d)
```
